```python
import math
import jax
import jax.numpy as jnp
from jax import lax
import numpy as np

D_MODEL = 1024
BATCH = 8
SEQ = 8192
DEPTH = 1
DEC_BATCH = 16
DEC_SEQ = 32
PAST_LEN = 2048

CHUNK = 64
D_SSM = D_MODEL // 2
SSM_GROUP_DIM = 16
SSM_GROUPS = D_SSM // SSM_GROUP_DIM
SSM_STATE = 64
D_POOL = D_MODEL // 2
POOL_WINDOWS = (2, 4, 8, 16)
POOL_GROUPS = len(POOL_WINDOWS)
POOL_GROUP_DIM = D_POOL // POOL_GROUPS
POOL_OUT_DIM = D_MODEL // POOL_GROUPS
POOL_MAXW = max(POOL_WINDOWS)
POOL_HIST = POOL_MAXW - 1
D_FF = 2816
D_IN_PROJ = D_SSM + D_POOL + 2 * D_MODEL
EPS = 1e-6
DT_MIN = 1e-3
DT_MAX = 1e-1

kernel_name = "gated_s5_pool_macaron_stream_step"


def rms_norm(x, g):
    xf = x.astype(jnp.float32)
    y = xf * lax.rsqrt(jnp.mean(xf * xf, axis=-1, keepdims=True) + EPS)
    return (y * g.astype(jnp.float32)).astype(x.dtype)


def swiglu(h, w_in, w_out):
    a, b = jnp.split(h @ w_in, 2, axis=-1)
    return (jax.nn.silu(a) * b) @ w_out


def _complex_scan_combine(e1, e2):
    a1r, a1i, b1r, b1i = e1
    a2r, a2i, b2r, b2i = e2
    return (a2r * a1r - a2i * a1i,
            a2r * a1i + a2i * a1r,
            a2r * b1r - a2i * b1i + b2r,
            a2r * b1i + a2i * b1r + b2i)


def s5_branch(u, h0_re, h0_im, lam_re, lam_im, log_step, b_re, b_im, c_re, c_im, d_skip, w_glu, b_glu):
    f32 = jnp.float32
    bsz, t_len, _ = u.shape
    uf = u.astype(f32)
    ug = uf.reshape(bsz, t_len, SSM_GROUPS, SSM_GROUP_DIM)
    lr = lam_re.astype(f32)
    li = lam_im.astype(f32)
    dt = jnp.exp(log_step.astype(f32))[:, None]
    mag = jnp.exp(lr * dt)
    ab_re = mag * jnp.cos(li * dt)
    ab_im = mag * jnp.sin(li * dt)
    den = lr * lr + li * li
    nr = ab_re - 1.0
    f_re = (nr * lr + ab_im * li) / den
    f_im = (ab_im * lr - nr * li) / den
    br = b_re.astype(f32)
    bi = b_im.astype(f32)
    bb_re = f_re[..., None] * br - f_im[..., None] * bi
    bb_im = f_re[..., None] * bi + f_im[..., None] * br
    v_re = jnp.einsum("btgc,gpc->tbgp", ug, bb_re)
    v_im = jnp.einsum("btgc,gpc->tbgp", ug, bb_im)
    if h0_re is not None:
        hr = h0_re.astype(f32)
        hi = h0_im.astype(f32)
        v_re = v_re.at[0].add(ab_re * hr - ab_im * hi)
        v_im = v_im.at[0].add(ab_re * hi + ab_im * hr)
    a_shape = (t_len, 1, SSM_GROUPS, SSM_STATE)
    a_re = jnp.broadcast_to(ab_re, a_shape)
    a_im = jnp.broadcast_to(ab_im, a_shape)
    _, _, s_re, s_im = lax.associative_scan(_complex_scan_combine, (a_re, a_im, v_re, v_im), axis=0)
    y = (jnp.einsum("tbgp,gcp->btgc", s_re, c_re.astype(f32))
         - jnp.einsum("tbgp,gcp->btgc", s_im, c_im.astype(f32)))
    y = y.reshape(bsz, t_len, D_SSM) + d_skip.astype(f32) * uf
    y = jax.nn.gelu(y).astype(u.dtype)
    a, g = jnp.split(y @ w_glu + b_glu, 2, axis=-1)
    return a * jax.nn.sigmoid(g), s_re[-1], s_im[-1]


def pool_branch(u, hist, pos0, pool_w, pool_scale):
    f32 = jnp.float32
    bsz, t_len, _ = u.shape
    n_hist = hist.shape[1]
    ext = jnp.concatenate([hist, u], axis=1).astype(f32)
    lead = jnp.zeros((bsz, POOL_MAXW, D_POOL), f32)
    cs = jnp.concatenate([lead, jnp.cumsum(ext, axis=1)], axis=1)
    start = POOL_MAXW + n_hist
    hi = cs[:, start:start + t_len]
    cur = ext[:, n_hist:]
    pos = pos0 + jnp.arange(t_len, dtype=jnp.int32)
    parts = []
    for gi, w in enumerate(POOL_WINDOWS):
        sl = slice(gi * POOL_GROUP_DIM, (gi + 1) * POOL_GROUP_DIM)
        lo = cs[:, start - w:start - w + t_len, sl]
        cnt = jnp.minimum(pos + 1, w).astype(f32)[None, :, None]
        parts.append((hi[..., sl] - lo) / cnt - cur[..., sl])
    pooled = jnp.stack(parts, axis=2)
    out = jnp.einsum("btgc,gco->btgo", pooled, pool_w.astype(f32))
    out = out.reshape(bsz, t_len, D_MODEL) * pool_scale.astype(f32)
    return out.astype(u.dtype)


def encoder_layer(x, h0_re, h0_im, hist, pos0, p):
    h = x + 0.5 * rms_norm(swiglu(rms_norm(x, p["ffn1_pre_g"]), p["ffn1_w_in"], p["ffn1_w_out"]), p["ffn1_post_g"])
    z = rms_norm(h, p["mix_pre_g"])
    proj = z @ p["w_in"] + p["b_in"]
    u_ssm = proj[..., :D_SSM]
    u_pool = proj[..., D_SSM:D_SSM + D_POOL]
    gates = jax.nn.sigmoid(proj[..., D_SSM + D_POOL:].astype(jnp.float32)).astype(x.dtype)
    g_ssm, g_pool = jnp.split(gates, 2, axis=-1)
    y_ssm, s_re, s_im = s5_branch(u_ssm, h0_re, h0_im, p["ssm_lambda_re"], p["ssm_lambda_im"], p["ssm_log_step"],
                                  p["ssm_b_re"], p["ssm_b_im"], p["ssm_c_re"], p["ssm_c_im"], p["ssm_d"],
                                  p["ssm_w_glu"], p["ssm_b_glu"])
    if hist is None:
        hist = u_pool[:, :0]
    y_pool = pool_branch(u_pool, hist, pos0, p["pool_w"], p["pool_scale"])
    merged = g_ssm * y_ssm + g_pool * y_pool
    h = h + rms_norm(merged @ p["w_out"], p["mix_post_g"])
    h = h + 0.5 * rms_norm(swiglu(rms_norm(h, p["ffn2_pre_g"]), p["ffn2_w_in"], p["ffn2_w_out"]), p["ffn2_post_g"])
    new_hist = jnp.concatenate([hist, u_pool], axis=1)[:, -POOL_HIST:]
    return h, s_re.astype(x.dtype), s_im.astype(x.dtype), new_hist


def setup_inputs(seed: int = 0) -> dict:
    key = jax.random.key(seed)
    ks = jax.random.split(key, 32)
    f32 = jnp.float32

    def nrm(k, shape, s):
        return s * jax.random.normal(k, shape, f32)

    def gain(k):
        return 1.0 + 0.05 * jax.random.normal(k, (DEPTH, D_MODEL), f32)

    n_idx = jnp.arange(SSM_STATE, dtype=f32)
    lam_re = -0.5 + 0.01 * jax.random.normal(ks[10], (DEPTH, SSM_GROUPS, SSM_STATE), f32)
    lam_im = jnp.pi * n_idx + 0.01 * jax.random.normal(ks[11], (DEPTH, SSM_GROUPS, SSM_STATE), f32)
    log_step = jax.random.uniform(ks[12], (DEPTH, SSM_GROUPS), f32, math.log(DT_MIN), math.log(DT_MAX))
    return {
        "x_prompt": nrm(ks[0], (BATCH, SEQ, D_MODEL), 1.0),
        "x_sample": nrm(ks[1], (DEC_BATCH, DEC_SEQ, D_MODEL), 1.0),
        "state_ssm_re": nrm(ks[2], (DEPTH, DEC_BATCH, SSM_GROUPS, SSM_STATE), 0.5),
        "state_ssm_im": nrm(ks[3], (DEPTH, DEC_BATCH, SSM_GROUPS, SSM_STATE), 0.5),
        "cache_pool": nrm(ks[4], (DEPTH, DEC_BATCH, POOL_HIST, D_POOL), 1.0),
        "ffn1_pre_g": gain(ks[5]),
        "ffn1_w_in": nrm(ks[6], (DEPTH, D_MODEL, 2 * D_FF), D_MODEL ** -0.5),
        "ffn1_w_out": nrm(ks[7], (DEPTH, D_FF, D_MODEL), D_FF ** -0.5),
        "ffn1_post_g": gain(ks[8]),
        "mix_pre_g": gain(ks[9]),
        "w_in": nrm(ks[13], (DEPTH, D_MODEL, D_IN_PROJ), D_MODEL ** -0.5),
        "b_in": nrm(ks[14], (DEPTH, D_IN_PROJ), 0.02),
        "ssm_lambda_re": lam_re,
        "ssm_lambda_im": lam_im,
        "ssm_log_step": log_step,
        "ssm_b_re": nrm(ks[15], (DEPTH, SSM_GROUPS, SSM_STATE, SSM_GROUP_DIM), (2 * SSM_GROUP_DIM) ** -0.5),
        "ssm_b_im": nrm(ks[16], (DEPTH, SSM_GROUPS, SSM_STATE, SSM_GROUP_DIM), (2 * SSM_GROUP_DIM) ** -0.5),
        "ssm_c_re": nrm(ks[17], (DEPTH, SSM_GROUPS, SSM_GROUP_DIM, SSM_STATE), (2 * SSM_STATE) ** -0.5),
        "ssm_c_im": nrm(ks[18], (DEPTH, SSM_GROUPS, SSM_GROUP_DIM, SSM_STATE), (2 * SSM_STATE) ** -0.5),
        "ssm_d": nrm(ks[19], (DEPTH, D_SSM), 1.0),
        "ssm_w_glu": nrm(ks[20], (DEPTH, D_SSM, 2 * D_MODEL), D_SSM ** -0.5),
        "ssm_b_glu": nrm(ks[21], (DEPTH, 2 * D_MODEL), 0.02),
        "pool_w": nrm(ks[22], (DEPTH, POOL_GROUPS, POOL_GROUP_DIM, POOL_OUT_DIM), POOL_GROUP_DIM ** -0.5),
        "pool_scale": 1.0 + 0.1 * jax.random.normal(ks[23], (DEPTH, D_MODEL), f32),
        "w_out": nrm(ks[24], (DEPTH, D_MODEL, D_MODEL), D_MODEL ** -0.5),
        "mix_post_g": gain(ks[25]),
        "ffn2_pre_g": gain(ks[26]),
        "ffn2_w_in": nrm(ks[27], (DEPTH, D_MODEL, 2 * D_FF), D_MODEL ** -0.5),
        "ffn2_w_out": nrm(ks[28], (DEPTH, D_FF, D_MODEL), D_FF ** -0.5),
        "ffn2_post_g": gain(ks[29]),
    }


def reference(x_prompt, x_sample, state_ssm_re, state_ssm_im, cache_pool,
              ffn1_pre_g, ffn1_w_in, ffn1_w_out, ffn1_post_g, mix_pre_g, w_in, b_in,
              ssm_lambda_re, ssm_lambda_im, ssm_log_step, ssm_b_re, ssm_b_im, ssm_c_re, ssm_c_im,
              ssm_d, ssm_w_glu, ssm_b_glu, pool_w, pool_scale, w_out, mix_post_g,
              ffn2_pre_g, ffn2_w_in, ffn2_w_out, ffn2_post_g):
    hp = x_prompt
    hs = x_sample
    pr_re, pr_im, pr_pool, sm_re, sm_im, sm_pool = [], [], [], [], [], []
    for l in range(DEPTH):
        p = {
            "ffn1_pre_g": ffn1_pre_g[l], "ffn1_w_in": ffn1_w_in[l], "ffn1_w_out": ffn1_w_out[l],
            "ffn1_post_g": ffn1_post_g[l], "mix_pre_g": mix_pre_g[l], "w_in": w_in[l], "b_in": b_in[l],
            "ssm_lambda_re": ssm_lambda_re[l], "ssm_lambda_im": ssm_lambda_im[l],
            "ssm_log_step": ssm_log_step[l], "ssm_b_re": ssm_b_re[l], "ssm_b_im": ssm_b_im[l],
            "ssm_c_re": ssm_c_re[l], "ssm_c_im": ssm_c_im[l], "ssm_d": ssm_d[l],
            "ssm_w_glu": ssm_w_glu[l], "ssm_b_glu": ssm_b_glu[l], "pool_w": pool_w[l],
            "pool_scale": pool_scale[l], "w_out": w_out[l], "mix_post_g": mix_post_g[l],
            "ffn2_pre_g": ffn2_pre_g[l], "ffn2_w_in": ffn2_w_in[l], "ffn2_w_out": ffn2_w_out[l],
            "ffn2_post_g": ffn2_post_g[l],
        }
        hp, s_re, s_im, s_pool = encoder_layer(hp, None, None, None, 0, p)
        pr_re.append(s_re)
        pr_im.append(s_im)
        pr_pool.append(s_pool)
        hs, s_re, s_im, s_pool = encoder_layer(hs, state_ssm_re[l], state_ssm_im[l], cache_pool[l], PAST_LEN, p)
        sm_re.append(s_re)
        sm_im.append(s_im)
        sm_pool.append(s_pool)
    return (hp, hs, jnp.stack(pr_re), jnp.stack(pr_im), jnp.stack(pr_pool),
            jnp.stack(sm_re), jnp.stack(sm_im), jnp.stack(sm_pool))
```

```python
import functools
import math

import jax
import jax.numpy as jnp
from jax import lax
from jax.experimental import pallas as pl
from jax.experimental.pallas import tpu as pltpu

F32 = jnp.float32
BF16 = jnp.bfloat16

D_MODEL = 1024
D_FF = 2816
D_SSM = 512
D_POOL = 512
SSM_GROUP_DIM = 16
SSM_GROUPS = 32
SSM_STATE = 64
POOL_WINDOWS = (2, 4, 8, 16)
POOL_GROUP_DIM = 128
POOL_OUT_DIM = 256
POOL_MAXW = 16
POOL_HIST = 15
EPS = 1e-6

SSM_BLOCKS = 4
GROUPS_PER_BLOCK = SSM_GROUPS // SSM_BLOCKS
BLOCK_CH = GROUPS_PER_BLOCK * SSM_GROUP_DIM
BLOCK_STATE = GROUPS_PER_BLOCK * SSM_STATE
STATE_COLS = 2 * SSM_GROUPS * SSM_STATE

FFN_CHUNK = 256
VMEM_LIMIT_BYTES = 56 * 1024 * 1024


def _rms(x, g):
    return x * lax.rsqrt(jnp.mean(x * x, axis=-1, keepdims=True) + EPS) * g


def _const_spec(shape):
    zeros = (0,) * len(shape)
    return pl.BlockSpec(shape, lambda i: zeros, pipeline_mode=pl.Buffered(1))


def _ffn_kernel(x_ref, pre_g_ref, w_in_ref, w_out_ref, post_g_ref, o_ref, xn_ref, acc_ref):
    x = x_ref[...]
    xn_ref[...] = _rms(x, pre_g_ref[...]).astype(BF16)
    acc_ref[...] = jnp.zeros_like(acc_ref)

    def chunk(j, carry):
        lo = pl.multiple_of(j * FFN_CHUNK, FFN_CHUNK)
        hi = pl.multiple_of(D_FF + j * FFN_CHUNK, FFN_CHUNK)
        xn = xn_ref[...]
        a = jnp.dot(xn, w_in_ref[:, pl.ds(lo, FFN_CHUNK)], preferred_element_type=F32)
        b = jnp.dot(xn, w_in_ref[:, pl.ds(hi, FFN_CHUNK)], preferred_element_type=F32)
        g = (a * jax.nn.sigmoid(a) * b).astype(BF16)
        acc_ref[...] += jnp.dot(g, w_out_ref[pl.ds(lo, FFN_CHUNK), :], preferred_element_type=F32)
        return carry

    lax.fori_loop(0, D_FF // FFN_CHUNK, chunk, 0)
    o_ref[...] = x + 0.5 * _rms(acc_ref[...], post_g_ref[...])


def _ffn(x, pre_g, w_in, w_out, post_g, *, tm, name):
    n = x.shape[0]
    return pl.pallas_call(
        _ffn_kernel,
        grid=(n // tm,),
        in_specs=[
            pl.BlockSpec((tm, D_MODEL), lambda i: (i, 0)),
            _const_spec((1, D_MODEL)),
            _const_spec((D_MODEL, 2 * D_FF)),
            _const_spec((D_FF, D_MODEL)),
            _const_spec((1, D_MODEL)),
        ],
        out_specs=pl.BlockSpec((tm, D_MODEL), lambda i: (i, 0)),
        out_shape=jax.ShapeDtypeStruct((n, D_MODEL), F32),
        scratch_shapes=[
            pltpu.VMEM((tm, D_MODEL), BF16),
            pltpu.VMEM((tm, D_MODEL), F32),
        ],
        compiler_params=pltpu.CompilerParams(
            dimension_semantics=("arbitrary",), vmem_limit_bytes=VMEM_LIMIT_BYTES),
        name=name,
    )(x, pre_g, w_in, w_out, post_g)


def _mix_kernel(pos0, nb, tc,
                h_ref, g_pre_ref, w_in_ref, b_in_ref, s0_ref, hist0_ref, a_ref, bm_ref, cm_ref,
                dskip_ref, w_glu_ref, b_glu_ref, pool_w_ref, pool_scale_ref, w_out_ref, g_post_ref,
                o_ref, s_out_ref, hist_out_ref,
                v_ref, ext_ref, st_ref):
    m_rows = nb * tc
    hist_rows = POOL_MAXW * nb
    step = pl.program_id(0)

    @pl.when(step == 0)
    def _():
        st_ref[...] = s0_ref[...]
        ext_ref[0:hist_rows, :] = hist0_ref[...]

    h = h_ref[...]
    z = _rms(h, g_pre_ref[...]).astype(BF16)

    u = jnp.dot(z, w_in_ref[:, 0:D_SSM + D_POOL], preferred_element_type=F32) + b_in_ref[:, 0:D_SSM + D_POOL]
    u_ssm = u[:, 0:D_SSM]
    ext_ref[hist_rows:hist_rows + m_rows, :] = u[:, D_SSM:D_SSM + D_POOL]

    u_ssm_bf = u_ssm.astype(BF16)
    for m in range(SSM_BLOCKS):
        v_ref[:, m * 2 * BLOCK_STATE:(m + 1) * 2 * BLOCK_STATE] = jnp.dot(
            u_ssm_bf[:, m * BLOCK_CH:(m + 1) * BLOCK_CH], bm_ref[m], preferred_element_type=F32)

    for half in range(2):
        c0 = half * 4 * BLOCK_STATE
        cols = [(c0 + k * 2 * BLOCK_STATE, c0 + k * 2 * BLOCK_STATE + BLOCK_STATE) for k in range(2)]
        a_re = [jnp.broadcast_to(a_ref[:, r:r + BLOCK_STATE], (nb, BLOCK_STATE)) for r, _ in cols]
        a_im = [jnp.broadcast_to(a_ref[:, i:i + BLOCK_STATE], (nb, BLOCK_STATE)) for _, i in cols]

        def scan_step(t, carry, cols=cols, a_re=a_re, a_im=a_im):
            row = pl.multiple_of(t * nb, nb)
            out = []
            for k, (r, i) in enumerate(cols):
                s_re, s_im = carry[2 * k], carry[2 * k + 1]
                n_re = a_re[k] * s_re - a_im[k] * s_im + v_ref[pl.ds(row, nb), r:r + BLOCK_STATE]
                n_im = a_re[k] * s_im + a_im[k] * s_re + v_ref[pl.ds(row, nb), i:i + BLOCK_STATE]
                v_ref[pl.ds(row, nb), r:r + BLOCK_STATE] = n_re
                v_ref[pl.ds(row, nb), i:i + BLOCK_STATE] = n_im
                out += [n_re, n_im]
            return tuple(out)

        init = []
        for r, i in cols:
            init += [st_ref[:, r:r + BLOCK_STATE], st_ref[:, i:i + BLOCK_STATE]]
        fin = lax.fori_loop(0, tc, scan_step, tuple(init))
        for k, (r, i) in enumerate(cols):
            st_ref[:, r:r + BLOCK_STATE] = fin[2 * k]
            st_ref[:, i:i + BLOCK_STATE] = fin[2 * k + 1]
    s_out_ref[...] = st_ref[...]

    y_parts = [
        jnp.dot(v_ref[:, m * 2 * BLOCK_STATE:(m + 1) * 2 * BLOCK_STATE].astype(BF16), cm_ref[m],
                preferred_element_type=F32)
        for m in range(SSM_BLOCKS)
    ]
    y = jnp.concatenate(y_parts, axis=-1) + dskip_ref[...] * u_ssm
    y = jax.nn.gelu(y, approximate=True).astype(BF16)
    glu = jnp.dot(y, w_glu_ref[...], preferred_element_type=F32) + b_glu_ref[...]
    y_ssm = glu[:, 0:D_MODEL] * jax.nn.sigmoid(glu[:, D_MODEL:2 * D_MODEL])

    row_id = lax.broadcasted_iota(jnp.int32, (m_rows, POOL_GROUP_DIM), 0)
    pos = pos0 + step * tc + row_id // nb
    pool_parts = []
    for gi, w in enumerate(POOL_WINDOWS):
        sl = slice(gi * POOL_GROUP_DIM, (gi + 1) * POOL_GROUP_DIM)
        cur = ext_ref[hist_rows:hist_rows + m_rows, sl]
        tot = cur
        for k in range(1, w):
            tot = tot + ext_ref[hist_rows - k * nb:hist_rows - k * nb + m_rows, sl]
        cnt = jnp.minimum(pos + 1, w).astype(F32)
        pooled = (tot / cnt - cur).astype(BF16)
        pool_parts.append(jnp.dot(pooled, pool_w_ref[gi], preferred_element_type=F32))
    y_pool = jnp.concatenate(pool_parts, axis=-1) * pool_scale_ref[...]

    gl = jnp.dot(z, w_in_ref[:, D_SSM + D_POOL:], preferred_element_type=F32) + b_in_ref[:, D_SSM + D_POOL:]
    gates = jax.nn.sigmoid(gl)
    merged = (gates[:, 0:D_MODEL] * y_ssm + gates[:, D_MODEL:2 * D_MODEL] * y_pool).astype(BF16)
    mixed = jnp.dot(merged, w_out_ref[...], preferred_element_type=F32)
    o_ref[...] = h + _rms(mixed, g_post_ref[...])

    tail = ext_ref[m_rows:m_rows + hist_rows, :]
    hist_out_ref[...] = tail
    ext_ref[0:hist_rows, :] = tail


def _mix(h, s0, hist0, p, *, pos0, nb, tc, name):
    n = h.shape[0]
    m_rows = nb * tc
    hist_rows = POOL_MAXW * nb
    kern = functools.partial(_mix_kernel, pos0, nb, tc)
    return pl.pallas_call(
        kern,
        grid=(n // m_rows,),
        in_specs=[
            pl.BlockSpec((m_rows, D_MODEL), lambda i: (i, 0)),
            _const_spec((1, D_MODEL)),
            _const_spec((D_MODEL, 3 * D_MODEL)),
            _const_spec((1, 3 * D_MODEL)),
            _const_spec((nb, STATE_COLS)),
            _const_spec((hist_rows, D_POOL)),
            _const_spec((1, STATE_COLS)),
            _const_spec((SSM_BLOCKS, BLOCK_CH, 2 * BLOCK_STATE)),
            _const_spec((SSM_BLOCKS, 2 * BLOCK_STATE, BLOCK_CH)),
            _const_spec((1, D_SSM)),
            _const_spec((D_SSM, 2 * D_MODEL)),
            _const_spec((1, 2 * D_MODEL)),
            _const_spec((len(POOL_WINDOWS), POOL_GROUP_DIM, POOL_OUT_DIM)),
            _const_spec((1, D_MODEL)),
            _const_spec((D_MODEL, D_MODEL)),
            _const_spec((1, D_MODEL)),
        ],
        out_specs=[
            pl.BlockSpec((m_rows, D_MODEL), lambda i: (i, 0)),
            pl.BlockSpec((nb, STATE_COLS), lambda i: (0, 0)),
            pl.BlockSpec((hist_rows, D_POOL), lambda i: (0, 0)),
        ],
        out_shape=[
            jax.ShapeDtypeStruct((n, D_MODEL), F32),
            jax.ShapeDtypeStruct((nb, STATE_COLS), F32),
            jax.ShapeDtypeStruct((hist_rows, D_POOL), F32),
        ],
        scratch_shapes=[
            pltpu.VMEM((m_rows, STATE_COLS), F32),
            pltpu.VMEM((hist_rows + m_rows, D_POOL), F32),
            pltpu.VMEM((nb, STATE_COLS), F32),
        ],
        compiler_params=pltpu.CompilerParams(
            dimension_semantics=("arbitrary",), vmem_limit_bytes=VMEM_LIMIT_BYTES),
        name=name,
    )(h, p["mix_pre_g"], p["w_in"], p["b_in"], s0, hist0, p["a"], p["bm"], p["cm"], p["ssm_d"],
      p["ssm_w_glu"], p["ssm_b_glu"], p["pool_w"], p["pool_scale"], p["w_out"], p["mix_post_g"])


def _block_diag(x):
    nblk, g, r, c = x.shape
    eye = jnp.eye(g, dtype=x.dtype)
    return jnp.einsum("mgrc,gh->mgrhc", x, eye).reshape(nblk, g * r, g * c)


def _ssm_params(lam_re, lam_im, log_step, b_re, b_im, c_re, c_im):
    dt = jnp.exp(log_step)[:, None]
    mag = jnp.exp(lam_re * dt)
    ab_re = mag * jnp.cos(lam_im * dt)
    ab_im = mag * jnp.sin(lam_im * dt)
    den = lam_re * lam_re + lam_im * lam_im
    nr = ab_re - 1.0
    f_re = (nr * lam_re + ab_im * lam_im) / den
    f_im = (ab_im * lam_re - nr * lam_im) / den
    bb_re = f_re[..., None] * b_re - f_im[..., None] * b_im
    bb_im = f_re[..., None] * b_im + f_im[..., None] * b_re

    def blk(x):
        return x.reshape((SSM_BLOCKS, GROUPS_PER_BLOCK) + x.shape[1:])

    bm = jnp.concatenate([
        _block_diag(jnp.swapaxes(blk(bb_re), 2, 3)),
        _block_diag(jnp.swapaxes(blk(bb_im), 2, 3)),
    ], axis=-1).astype(BF16)
    cm = jnp.concatenate([
        _block_diag(jnp.swapaxes(blk(c_re), 2, 3)),
        _block_diag(jnp.swapaxes(blk(-c_im), 2, 3)),
    ], axis=-2).astype(BF16)
    a = jnp.concatenate([blk(ab_re).reshape(SSM_BLOCKS, BLOCK_STATE),
                         blk(ab_im).reshape(SSM_BLOCKS, BLOCK_STATE)], axis=-1).reshape(1, STATE_COLS)
    return a, bm, cm


def _state_to_cols(s_re, s_im):
    nb = s_re.shape[0]
    re = s_re.reshape(nb, SSM_BLOCKS, BLOCK_STATE)
    im = s_im.reshape(nb, SSM_BLOCKS, BLOCK_STATE)
    return jnp.stack([re, im], axis=2).reshape(nb, STATE_COLS)


def _cols_to_state(s):
    nb = s.shape[0]
    s = s.reshape(nb, SSM_BLOCKS, 2, GROUPS_PER_BLOCK, SSM_STATE)
    re = s[:, :, 0].reshape(nb, SSM_GROUPS, SSM_STATE)
    im = s[:, :, 1].reshape(nb, SSM_GROUPS, SSM_STATE)
    return re, im


def _layer(x, s0, hist0, p, *, pos0, tm, tc):
    nb, t_len, _ = x.shape
    xt = jnp.swapaxes(x, 0, 1).reshape(t_len * nb, D_MODEL)
    h = _ffn(xt, p["ffn1_pre_g"], p["ffn1_w_in"], p["ffn1_w_out"], p["ffn1_post_g"], tm=tm, name="ffn1")
    h, s_fin, hist = _mix(h, s0, hist0, p, pos0=pos0, nb=nb, tc=tc, name="mix")
    h = _ffn(h, p["ffn2_pre_g"], p["ffn2_w_in"], p["ffn2_w_out"], p["ffn2_post_g"], tm=tm, name="ffn2")
    y = jnp.swapaxes(h.reshape(t_len, nb, D_MODEL), 0, 1)
    s_re, s_im = _cols_to_state(s_fin)
    pool = jnp.swapaxes(hist.reshape(POOL_MAXW, nb, D_POOL)[1:], 0, 1)
    return y, s_re, s_im, pool


def kernel(x_prompt, x_sample, state_ssm_re, state_ssm_im, cache_pool, ffn1_pre_g, ffn1_w_in, ffn1_w_out, ffn1_post_g, mix_pre_g, w_in, b_in, ssm_lambda_re, ssm_lambda_im, ssm_log_step, ssm_b_re, ssm_b_im, ssm_c_re, ssm_c_im, ssm_d, ssm_w_glu, ssm_b_glu, pool_w, pool_scale, w_out, mix_post_g, ffn2_pre_g, ffn2_w_in, ffn2_w_out, ffn2_post_g):
    depth = ffn1_w_in.shape[0]
    nb_p, t_p, _ = x_prompt.shape
    nb_s, t_s, _ = x_sample.shape
    past_len = 2048

    hp, hs = x_prompt, x_sample
    outs = [[] for _ in range(6)]
    for l in range(depth):
        a, bm, cm = _ssm_params(ssm_lambda_re[l], ssm_lambda_im[l], ssm_log_step[l],
                                ssm_b_re[l], ssm_b_im[l], ssm_c_re[l], ssm_c_im[l])
        p = {
            "ffn1_pre_g": ffn1_pre_g[l][None], "ffn1_w_in": ffn1_w_in[l].astype(BF16),
            "ffn1_w_out": ffn1_w_out[l].astype(BF16), "ffn1_post_g": ffn1_post_g[l][None],
            "mix_pre_g": mix_pre_g[l][None], "w_in": w_in[l].astype(BF16), "b_in": b_in[l][None],
            "a": a, "bm": bm, "cm": cm, "ssm_d": ssm_d[l][None],
            "ssm_w_glu": ssm_w_glu[l].astype(BF16), "ssm_b_glu": ssm_b_glu[l][None],
            "pool_w": pool_w[l].astype(BF16), "pool_scale": pool_scale[l][None],
            "w_out": w_out[l].astype(BF16), "mix_post_g": mix_post_g[l][None],
            "ffn2_pre_g": ffn2_pre_g[l][None], "ffn2_w_in": ffn2_w_in[l].astype(BF16),
            "ffn2_w_out": ffn2_w_out[l].astype(BF16), "ffn2_post_g": ffn2_post_g[l][None],
        }
        s0 = jnp.zeros((nb_p, STATE_COLS), F32)
        hist0 = jnp.zeros((POOL_MAXW * nb_p, D_POOL), F32)
        hp, s_re, s_im, pool = _layer(hp, s0, hist0, p, pos0=0, tm=1024, tc=64)
        for lst, val in zip(outs[:3], (s_re, s_im, pool)):
            lst.append(val)
        s0 = _state_to_cols(state_ssm_re[l], state_ssm_im[l])
        hist0 = jnp.pad(jnp.swapaxes(cache_pool[l], 0, 1), ((1, 0), (0, 0), (0, 0))).reshape(
            POOL_MAXW * nb_s, D_POOL)
        hs, s_re, s_im, pool = _layer(hs, s0, hist0, p, pos0=past_len, tm=nb_s * t_s, tc=t_s)
        for lst, val in zip(outs[3:], (s_re, s_im, pool)):
            lst.append(val)
    return (hp, hs) + tuple(jnp.stack(o) for o in outs)
```

```python
import functools

import jax
import jax.numpy as jnp
from jax import lax
from jax.experimental import pallas as pl
from jax.experimental.pallas import tpu as pltpu

F32 = jnp.float32
BF16 = jnp.bfloat16

LANES = 128
D_MODEL = 1024
D_FF = 2816
D_SSM = 512
D_POOL = 512
SSM_GROUP_DIM = 16
SSM_GROUPS = 32
SSM_STATE = 64
POOL_WINDOWS = (2, 4, 8, 16)
POOL_GROUP_DIM = 128
POOL_OUT_DIM = 256
POOL_MAXW = 16
PAST_LEN = 2048
EPS = 1e-6

N_SLABS = D_MODEL // LANES

SSM_BLOCKS = 4
GROUPS_PER_BLOCK = SSM_GROUPS // SSM_BLOCKS
BLOCK_CH = GROUPS_PER_BLOCK * SSM_GROUP_DIM
BLOCK_STATE = GROUPS_PER_BLOCK * SSM_STATE
STATE_COLS = 2 * SSM_GROUPS * SSM_STATE

FFN_CHUNK = 256
VMEM_LIMIT_BYTES = 56 * 1024 * 1024


def _rms(x, g):
    return x * lax.rsqrt(jnp.mean(x * x, axis=-1, keepdims=True) + EPS) * g


def _const_spec(shape):
    zeros = (0,) * len(shape)
    return pl.BlockSpec(shape, lambda i: zeros, pipeline_mode=pl.Buffered(1))


def _ffn_body(x, pre_g_ref, w_in_ref, w_out_ref, post_g_ref, xn_ref, g_ref):
    xn_ref[...] = _rms(x, pre_g_ref[...]).astype(BF16)
    for j in range(D_FF // FFN_CHUNK):
        lo, hi = j * FFN_CHUNK, D_FF + j * FFN_CHUNK
        xn = xn_ref[...]
        a = jnp.dot(xn, w_in_ref[:, lo:lo + FFN_CHUNK], preferred_element_type=F32)
        b = jnp.dot(xn, w_in_ref[:, hi:hi + FFN_CHUNK], preferred_element_type=F32)
        g_ref[:, lo:lo + FFN_CHUNK] = (a * jax.nn.sigmoid(a) * b).astype(BF16)
    y = jnp.dot(g_ref[...], w_out_ref[...], preferred_element_type=F32)
    return x + 0.5 * _rms(y, post_g_ref[...])


def _ffn_in_kernel(nb, tt, x_ref, pre_g_ref, w_in_ref, w_out_ref, post_g_ref, o_ref, xn_ref, g_ref):
    x = x_ref[...].reshape(nb * tt, D_MODEL)
    o = _ffn_body(x, pre_g_ref, w_in_ref, w_out_ref, post_g_ref, xn_ref, g_ref)
    for j in range(N_SLABS):
        for b in range(nb):
            o_ref[j, pl.ds(b, tt, stride=nb), :] = o[b * tt:(b + 1) * tt, j * LANES:(j + 1) * LANES]


def _ffn_out_kernel(nb, tt, x_ref, pre_g_ref, w_in_ref, w_out_ref, post_g_ref, o_ref, xn_ref, g_ref):
    x = jnp.concatenate([
        jnp.concatenate([x_ref[j, pl.ds(b, tt, stride=nb), :] for b in range(nb)], axis=0)
        for j in range(N_SLABS)], axis=1)
    o = _ffn_body(x, pre_g_ref, w_in_ref, w_out_ref, post_g_ref, xn_ref, g_ref)
    o_ref[...] = o.reshape(nb, tt, D_MODEL)


def _ffn(x, pre_g, w_in, w_out, post_g, *, nb, t_len, tt, rows_in, name):
    tm = nb * tt
    natural = pl.BlockSpec((nb, tt, D_MODEL), lambda i: (0, i, 0))
    slabs = pl.BlockSpec((N_SLABS, tm, LANES), lambda i: (0, i, 0))
    natural_shape = jax.ShapeDtypeStruct((nb, t_len, D_MODEL), F32)
    slabs_shape = jax.ShapeDtypeStruct((N_SLABS, t_len * nb, LANES), F32)
    kern = functools.partial(_ffn_in_kernel if rows_in else _ffn_out_kernel, nb, tt)
    return pl.pallas_call(
        kern,
        grid=(t_len // tt,),
        in_specs=[
            natural if rows_in else slabs,
            _const_spec((1, D_MODEL)),
            _const_spec((D_MODEL, 2 * D_FF)),
            _const_spec((D_FF, D_MODEL)),
            _const_spec((1, D_MODEL)),
        ],
        out_specs=slabs if rows_in else natural,
        out_shape=slabs_shape if rows_in else natural_shape,
        scratch_shapes=[
            pltpu.VMEM((tm, D_MODEL), BF16),
            pltpu.VMEM((tm, D_FF), BF16),
        ],
        compiler_params=pltpu.CompilerParams(
            dimension_semantics=("arbitrary",), vmem_limit_bytes=VMEM_LIMIT_BYTES),
        name=name,
    )(x, pre_g, w_in, w_out, post_g)


def _mix_kernel(pos0, nb, tc,
                h_ref, g_pre_ref, w_in_ref, b_in_ref, s0_ref, hist0_ref, a_ref, bm_ref, cm_ref,
                dskip_ref, w_glu_ref, b_glu_ref, pool_w_ref, pool_scale_ref, w_out_ref, g_post_ref,
                o_ref, s_out_ref, hist_out_ref,
                v_ref, ext_ref, st_ref):
    m_rows = nb * tc
    hist_rows = POOL_MAXW * nb
    step = pl.program_id(0)

    @pl.when(step == 0)
    def _():
        st_ref[...] = s0_ref[...]
        ext_ref[0:hist_rows, :] = hist0_ref[...]

    h = jnp.concatenate([h_ref[j] for j in range(N_SLABS)], axis=1)
    z = _rms(h, g_pre_ref[...]).astype(BF16)

    u = jnp.dot(z, w_in_ref[:, 0:D_SSM + D_POOL], preferred_element_type=F32) + b_in_ref[:, 0:D_SSM + D_POOL]
    u_ssm = u[:, 0:D_SSM]
    ext_ref[hist_rows:hist_rows + m_rows, :] = u[:, D_SSM:D_SSM + D_POOL]

    u_ssm_bf = u_ssm.astype(BF16)
    for m in range(SSM_BLOCKS):
        v_ref[:, m * 2 * BLOCK_STATE:(m + 1) * 2 * BLOCK_STATE] = jnp.dot(
            u_ssm_bf[:, m * BLOCK_CH:(m + 1) * BLOCK_CH], bm_ref[m], preferred_element_type=F32)

    for m in range(SSM_BLOCKS):
        r0 = m * 2 * BLOCK_STATE
        i0 = r0 + BLOCK_STATE
        a_re = jnp.broadcast_to(a_ref[:, r0:r0 + BLOCK_STATE], (nb, BLOCK_STATE))
        a_im = jnp.broadcast_to(a_ref[:, i0:i0 + BLOCK_STATE], (nb, BLOCK_STATE))
        s_re = st_ref[:, r0:r0 + BLOCK_STATE]
        s_im = st_ref[:, i0:i0 + BLOCK_STATE]
        for t in range(tc):
            rows = slice(t * nb, (t + 1) * nb)
            n_re = a_re * s_re - a_im * s_im + v_ref[rows, r0:r0 + BLOCK_STATE]
            n_im = a_re * s_im + a_im * s_re + v_ref[rows, i0:i0 + BLOCK_STATE]
            v_ref[rows, r0:r0 + BLOCK_STATE] = n_re
            v_ref[rows, i0:i0 + BLOCK_STATE] = n_im
            s_re, s_im = n_re, n_im
        st_ref[:, r0:r0 + BLOCK_STATE] = s_re
        st_ref[:, i0:i0 + BLOCK_STATE] = s_im
    s_out_ref[...] = st_ref[...]

    y_parts = [
        jnp.dot(v_ref[:, m * 2 * BLOCK_STATE:(m + 1) * 2 * BLOCK_STATE].astype(BF16), cm_ref[m],
                preferred_element_type=F32)
        for m in range(SSM_BLOCKS)
    ]
    y = jnp.concatenate(y_parts, axis=-1) + dskip_ref[...] * u_ssm
    y = jax.nn.gelu(y, approximate=True).astype(BF16)
    glu = jnp.dot(y, w_glu_ref[...], preferred_element_type=F32) + b_glu_ref[...]
    y_ssm = glu[:, 0:D_MODEL] * jax.nn.sigmoid(glu[:, D_MODEL:2 * D_MODEL])

    row_id = lax.broadcasted_iota(jnp.int32, (m_rows, POOL_GROUP_DIM), 0)
    pos = pos0 + step * tc + row_id // nb
    pool_parts = []
    for gi, w in enumerate(POOL_WINDOWS):
        sl = slice(gi * POOL_GROUP_DIM, (gi + 1) * POOL_GROUP_DIM)
        cur = ext_ref[hist_rows:hist_rows + m_rows, sl]
        tot = cur
        for k in range(1, w):
            tot = tot + ext_ref[hist_rows - k * nb:hist_rows - k * nb + m_rows, sl]
        cnt = jnp.minimum(pos + 1, w).astype(F32)
        pooled = (tot / cnt - cur).astype(BF16)
        pool_parts.append(jnp.dot(pooled, pool_w_ref[gi], preferred_element_type=F32))
    y_pool = jnp.concatenate(pool_parts, axis=-1) * pool_scale_ref[...]

    gl = jnp.dot(z, w_in_ref[:, D_SSM + D_POOL:], preferred_element_type=F32) + b_in_ref[:, D_SSM + D_POOL:]
    gates = jax.nn.sigmoid(gl)
    merged = (gates[:, 0:D_MODEL] * y_ssm + gates[:, D_MODEL:2 * D_MODEL] * y_pool).astype(BF16)
    mixed = jnp.dot(merged, w_out_ref[...], preferred_element_type=F32)
    o = h + _rms(mixed, g_post_ref[...])
    for j in range(N_SLABS):
        o_ref[j] = o[:, j * LANES:(j + 1) * LANES]

    tail = ext_ref[m_rows:m_rows + hist_rows, :]
    hist_out_ref[...] = tail
    ext_ref[0:hist_rows, :] = tail


def _mix(h, s0, hist0, p, *, pos0, nb, tc, name):
    n = h.shape[1]
    m_rows = nb * tc
    hist_rows = POOL_MAXW * nb
    kern = functools.partial(_mix_kernel, pos0, nb, tc)
    slabs = pl.BlockSpec((N_SLABS, m_rows, LANES), lambda i: (0, i, 0))
    return pl.pallas_call(
        kern,
        grid=(n // m_rows,),
        in_specs=[
            slabs,
            _const_spec((1, D_MODEL)),
            _const_spec((D_MODEL, 3 * D_MODEL)),
            _const_spec((1, 3 * D_MODEL)),
            _const_spec((nb, STATE_COLS)),
            _const_spec((hist_rows, D_POOL)),
            _const_spec((1, STATE_COLS)),
            _const_spec((SSM_BLOCKS, BLOCK_CH, 2 * BLOCK_STATE)),
            _const_spec((SSM_BLOCKS, 2 * BLOCK_STATE, BLOCK_CH)),
            _const_spec((1, D_SSM)),
            _const_spec((D_SSM, 2 * D_MODEL)),
            _const_spec((1, 2 * D_MODEL)),
            _const_spec((len(POOL_WINDOWS), POOL_GROUP_DIM, POOL_OUT_DIM)),
            _const_spec((1, D_MODEL)),
            _const_spec((D_MODEL, D_MODEL)),
            _const_spec((1, D_MODEL)),
        ],
        out_specs=[
            slabs,
            pl.BlockSpec((nb, STATE_COLS), lambda i: (0, 0)),
            pl.BlockSpec((hist_rows, D_POOL), lambda i: (0, 0)),
        ],
        out_shape=[
            jax.ShapeDtypeStruct((N_SLABS, n, LANES), F32),
            jax.ShapeDtypeStruct((nb, STATE_COLS), F32),
            jax.ShapeDtypeStruct((hist_rows, D_POOL), F32),
        ],
        scratch_shapes=[
            pltpu.VMEM((m_rows, STATE_COLS), F32),
            pltpu.VMEM((hist_rows + m_rows, D_POOL), F32),
            pltpu.VMEM((nb, STATE_COLS), F32),
        ],
        compiler_params=pltpu.CompilerParams(
            dimension_semantics=("arbitrary",), vmem_limit_bytes=VMEM_LIMIT_BYTES),
        name=name,
    )(h, p["mix_pre_g"], p["w_in"], p["b_in"], s0, hist0, p["a"], p["bm"], p["cm"], p["ssm_d"],
      p["ssm_w_glu"], p["ssm_b_glu"], p["pool_w"], p["pool_scale"], p["w_out"], p["mix_post_g"])


def _block_diag(x):
    nblk, g, r, c = x.shape
    eye = jnp.eye(g, dtype=x.dtype)
    return jnp.einsum("mgrc,gh->mgrhc", x, eye).reshape(nblk, g * r, g * c)


def _ssm_params(lam_re, lam_im, log_step, b_re, b_im, c_re, c_im):
    dt = jnp.exp(log_step)[:, None]
    mag = jnp.exp(lam_re * dt)
    ab_re = mag * jnp.cos(lam_im * dt)
    ab_im = mag * jnp.sin(lam_im * dt)
    den = lam_re * lam_re + lam_im * lam_im
    nr = ab_re - 1.0
    f_re = (nr * lam_re + ab_im * lam_im) / den
    f_im = (ab_im * lam_re - nr * lam_im) / den
    bb_re = f_re[..., None] * b_re - f_im[..., None] * b_im
    bb_im = f_re[..., None] * b_im + f_im[..., None] * b_re

    def blk(x):
        return x.reshape((SSM_BLOCKS, GROUPS_PER_BLOCK) + x.shape[1:])

    bm = jnp.concatenate([
        _block_diag(jnp.swapaxes(blk(bb_re), 2, 3)),
        _block_diag(jnp.swapaxes(blk(bb_im), 2, 3)),
    ], axis=-1).astype(BF16)
    cm = jnp.concatenate([
        _block_diag(jnp.swapaxes(blk(c_re), 2, 3)),
        _block_diag(jnp.swapaxes(blk(-c_im), 2, 3)),
    ], axis=-2).astype(BF16)
    a = jnp.concatenate([blk(ab_re).reshape(SSM_BLOCKS, BLOCK_STATE),
                         blk(ab_im).reshape(SSM_BLOCKS, BLOCK_STATE)], axis=-1).reshape(1, STATE_COLS)
    return a, bm, cm


def _state_to_cols(s_re, s_im):
    nb = s_re.shape[0]
    re = s_re.reshape(nb, SSM_BLOCKS, BLOCK_STATE)
    im = s_im.reshape(nb, SSM_BLOCKS, BLOCK_STATE)
    return jnp.stack([re, im], axis=2).reshape(nb, STATE_COLS)


def _cols_to_state(s):
    nb = s.shape[0]
    s = s.reshape(nb, SSM_BLOCKS, 2, GROUPS_PER_BLOCK, SSM_STATE)
    re = s[:, :, 0].reshape(nb, SSM_GROUPS, SSM_STATE)
    im = s[:, :, 1].reshape(nb, SSM_GROUPS, SSM_STATE)
    return re, im


def _layer(x, s0, hist0, p, *, pos0, tt, tc):
    nb, t_len, _ = x.shape
    h = _ffn(x, p["ffn1_pre_g"], p["ffn1_w_in"], p["ffn1_w_out"], p["ffn1_post_g"],
             nb=nb, t_len=t_len, tt=tt, rows_in=True, name="ffn1")
    h, s_fin, hist = _mix(h, s0, hist0, p, pos0=pos0, nb=nb, tc=tc, name="mix")
    y = _ffn(h, p["ffn2_pre_g"], p["ffn2_w_in"], p["ffn2_w_out"], p["ffn2_post_g"],
             nb=nb, t_len=t_len, tt=tt, rows_in=False, name="ffn2")
    s_re, s_im = _cols_to_state(s_fin)
    pool = jnp.swapaxes(hist.reshape(POOL_MAXW, nb, D_POOL)[1:], 0, 1)
    return y, s_re, s_im, pool


def kernel(x_prompt, x_sample, state_ssm_re, state_ssm_im, cache_pool, ffn1_pre_g, ffn1_w_in, ffn1_w_out, ffn1_post_g, mix_pre_g, w_in, b_in, ssm_lambda_re, ssm_lambda_im, ssm_log_step, ssm_b_re, ssm_b_im, ssm_c_re, ssm_c_im, ssm_d, ssm_w_glu, ssm_b_glu, pool_w, pool_scale, w_out, mix_post_g, ffn2_pre_g, ffn2_w_in, ffn2_w_out, ffn2_post_g):
    depth = ffn1_w_in.shape[0]
    nb_p = x_prompt.shape[0]
    nb_s, t_s, _ = x_sample.shape

    hp, hs = x_prompt, x_sample
    outs = [[] for _ in range(6)]
    for l in range(depth):
        a, bm, cm = _ssm_params(ssm_lambda_re[l], ssm_lambda_im[l], ssm_log_step[l],
                                ssm_b_re[l], ssm_b_im[l], ssm_c_re[l], ssm_c_im[l])
        p = {
            "ffn1_pre_g": ffn1_pre_g[l][None], "ffn1_w_in": ffn1_w_in[l].astype(BF16),
            "ffn1_w_out": ffn1_w_out[l].astype(BF16), "ffn1_post_g": ffn1_post_g[l][None],
            "mix_pre_g": mix_pre_g[l][None], "w_in": w_in[l].astype(BF16), "b_in": b_in[l][None],
            "a": a, "bm": bm, "cm": cm, "ssm_d": ssm_d[l][None],
            "ssm_w_glu": ssm_w_glu[l].astype(BF16), "ssm_b_glu": ssm_b_glu[l][None],
            "pool_w": pool_w[l].astype(BF16), "pool_scale": pool_scale[l][None],
            "w_out": w_out[l].astype(BF16), "mix_post_g": mix_post_g[l][None],
            "ffn2_pre_g": ffn2_pre_g[l][None], "ffn2_w_in": ffn2_w_in[l].astype(BF16),
            "ffn2_w_out": ffn2_w_out[l].astype(BF16), "ffn2_post_g": ffn2_post_g[l][None],
        }
        s0 = jnp.zeros((nb_p, STATE_COLS), F32)
        hist0 = jnp.zeros((POOL_MAXW * nb_p, D_POOL), F32)
        hp, s_re, s_im, pool = _layer(hp, s0, hist0, p, pos0=0, tt=128, tc=64)
        for lst, val in zip(outs[:3], (s_re, s_im, pool)):
            lst.append(val)
        s0 = _state_to_cols(state_ssm_re[l], state_ssm_im[l])
        hist0 = jnp.pad(jnp.swapaxes(cache_pool[l], 0, 1), ((1, 0), (0, 0), (0, 0))).reshape(
            POOL_MAXW * nb_s, D_POOL)
        hs, s_re, s_im, pool = _layer(hs, s0, hist0, p, pos0=PAST_LEN, tt=t_s, tc=t_s)
        for lst, val in zip(outs[3:], (s_re, s_im, pool)):
            lst.append(val)
    return (hp, hs) + tuple(jnp.stack(o) for o in outs)
```

```python
import functools

import jax
import jax.numpy as jnp
from jax import lax
from jax.experimental import pallas as pl
from jax.experimental.pallas import tpu as pltpu

F32 = jnp.float32
BF16 = jnp.bfloat16

LANES = 128
D_MODEL = 1024
D_FF = 2816
D_SSM = 512
D_POOL = 512
SSM_GROUP_DIM = 16
SSM_GROUPS = 32
SSM_STATE = 64
POOL_WINDOWS = (2, 4, 8, 16)
POOL_GROUP_DIM = 128
POOL_OUT_DIM = 256
POOL_MAXW = 16
PAST_LEN = 2048
EPS = 1e-6

N_SLABS = D_MODEL // LANES

SSM_BLOCKS = 4
GROUPS_PER_BLOCK = SSM_GROUPS // SSM_BLOCKS
BLOCK_CH = GROUPS_PER_BLOCK * SSM_GROUP_DIM
BLOCK_STATE = GROUPS_PER_BLOCK * SSM_STATE
STATE_COLS = 2 * SSM_GROUPS * SSM_STATE

FFN_CHUNK = 256
FFN_SUBTILES = 2
MIX_SUBTILES = 2
GATE_CHUNK = 256
VMEM_LIMIT_BYTES = 56 * 1024 * 1024


def _rms(x, g):
    return x * lax.rsqrt(jnp.mean(x * x, axis=-1, keepdims=True) + EPS) * g


def _const_spec(shape):
    zeros = (0,) * len(shape)
    return pl.BlockSpec(shape, lambda i: zeros, pipeline_mode=pl.Buffered(1))


def _ffn_body(x, pre_g_ref, w_in_ref, w_out_ref, half_post_g_ref, xn_ref, g_ref):
    rows = x.shape[0]
    sub = rows // FFN_SUBTILES
    outs = []
    for s in range(FFN_SUBTILES):
        r = slice(s * sub, (s + 1) * sub)
        xs = x[r]
        xn_ref[r, :] = _rms(xs, pre_g_ref[...]).astype(BF16)
        for j in range(D_FF // FFN_CHUNK):
            lo, hi = j * FFN_CHUNK, D_FF + j * FFN_CHUNK
            xn = xn_ref[r, :]
            a = jnp.dot(xn, w_in_ref[:, lo:lo + FFN_CHUNK], preferred_element_type=F32)
            b = jnp.dot(xn, w_in_ref[:, hi:hi + FFN_CHUNK], preferred_element_type=F32)
            g_ref[r, lo:lo + FFN_CHUNK] = (a * jax.nn.sigmoid(a) * b).astype(BF16)
        y = jnp.dot(g_ref[r, :], w_out_ref[...], preferred_element_type=F32)
        outs.append(xs + _rms(y, half_post_g_ref[...]))
    return jnp.concatenate(outs, axis=0)


def _ffn_in_kernel(nb, tt, x_ref, pre_g_ref, w_in_ref, w_out_ref, half_post_g_ref, o_ref, xn_ref, g_ref):
    x = x_ref[...].reshape(nb * tt, D_MODEL)
    o = _ffn_body(x, pre_g_ref, w_in_ref, w_out_ref, half_post_g_ref, xn_ref, g_ref)
    for j in range(N_SLABS):
        for b in range(nb):
            o_ref[j, pl.ds(b, tt, stride=nb), :] = o[b * tt:(b + 1) * tt, j * LANES:(j + 1) * LANES]


def _ffn_out_kernel(nb, tt, x_ref, pre_g_ref, w_in_ref, w_out_ref, half_post_g_ref, o_ref, xn_ref, g_ref):
    x = jnp.concatenate([
        jnp.concatenate([x_ref[j, pl.ds(b, tt, stride=nb), :] for b in range(nb)], axis=0)
        for j in range(N_SLABS)], axis=1)
    o = _ffn_body(x, pre_g_ref, w_in_ref, w_out_ref, half_post_g_ref, xn_ref, g_ref)
    o_ref[...] = o.reshape(nb, tt, D_MODEL)


def _ffn(x, pre_g, w_in, w_out, half_post_g, *, nb, t_len, tt, rows_in, name):
    tm = nb * tt
    natural = pl.BlockSpec((nb, tt, D_MODEL), lambda i: (0, i, 0))
    slabs = pl.BlockSpec((N_SLABS, tm, LANES), lambda i: (0, i, 0))
    natural_shape = jax.ShapeDtypeStruct((nb, t_len, D_MODEL), F32)
    slabs_shape = jax.ShapeDtypeStruct((N_SLABS, t_len * nb, LANES), F32)
    kern = functools.partial(_ffn_in_kernel if rows_in else _ffn_out_kernel, nb, tt)
    return pl.pallas_call(
        kern,
        grid=(t_len // tt,),
        in_specs=[
            natural if rows_in else slabs,
            _const_spec((1, D_MODEL)),
            _const_spec((D_MODEL, 2 * D_FF)),
            _const_spec((D_FF, D_MODEL)),
            _const_spec((1, D_MODEL)),
        ],
        out_specs=slabs if rows_in else natural,
        out_shape=slabs_shape if rows_in else natural_shape,
        scratch_shapes=[
            pltpu.VMEM((tm, D_MODEL), BF16),
            pltpu.VMEM((tm, D_FF), BF16),
        ],
        compiler_params=pltpu.CompilerParams(
            dimension_semantics=("arbitrary",), vmem_limit_bytes=VMEM_LIMIT_BYTES),
        name=name,
    )(x, pre_g, w_in, w_out, half_post_g)


def _mix_kernel(pos0, nb, tc,
                h_ref, g_pre_ref, w_in_ref, b_in_ref, s0_ref, hist0_ref, a_ref, bm_ref, cm_ref,
                dskip_ref, w_glu_ref, b_glu_ref, pool_w_ref, pool_scale_ref, w_out_ref, g_post_ref,
                o_ref, s_out_ref, hist_out_ref,
                vg_ref, ext_ref, st_ref):
    m_rows = nb * tc
    hist_rows = POOL_MAXW * nb
    step = pl.program_id(0)
    half_cols = 2 * 2 * BLOCK_STATE
    gate_rows = slice(2 * m_rows, 3 * m_rows)

    @pl.when(step == 0)
    def _():
        st_ref[...] = s0_ref[...]
        ext_ref[0:hist_rows, :] = hist0_ref[...]

    h = jnp.concatenate([h_ref[j] for j in range(N_SLABS)], axis=1)
    z = _rms(h, g_pre_ref[...]).astype(BF16)

    u = jnp.dot(z, w_in_ref[:, 0:D_SSM + D_POOL], preferred_element_type=F32) + b_in_ref[:, 0:D_SSM + D_POOL]
    u_ssm = u[:, 0:D_SSM]
    ext_ref[hist_rows:hist_rows + m_rows, :] = u[:, D_SSM:D_SSM + D_POOL]

    def v_block(m):
        half, k = divmod(m, 2)
        return slice(half * m_rows, (half + 1) * m_rows), slice(k * 2 * BLOCK_STATE, (k + 1) * 2 * BLOCK_STATE)

    u_ssm_bf = u_ssm.astype(BF16)
    for m in range(SSM_BLOCKS):
        vg_ref[v_block(m)] = jnp.dot(
            u_ssm_bf[:, m * BLOCK_CH:(m + 1) * BLOCK_CH], bm_ref[m], preferred_element_type=F32)

    def gate(c):
        lo = D_SSM + D_POOL + c * GATE_CHUNK
        gl = jnp.dot(z, w_in_ref[:, lo:lo + GATE_CHUNK], preferred_element_type=F32) + b_in_ref[:, lo:lo + GATE_CHUNK]
        return jax.nn.sigmoid(gl)

    def ssm_gate_task(c):
        def run():
            vg_ref[gate_rows, c * GATE_CHUNK:(c + 1) * GATE_CHUNK] = gate(c)
        return run

    def pool_task(gi, w):
        def run():
            sl = slice(gi * POOL_GROUP_DIM, (gi + 1) * POOL_GROUP_DIM)
            row_id = lax.broadcasted_iota(jnp.int32, (m_rows, POOL_GROUP_DIM), 0)
            pos = pos0 + step * tc + row_id // nb
            cur = ext_ref[hist_rows:hist_rows + m_rows, sl]
            tot = cur
            for k in range(1, w):
                tot = tot + ext_ref[hist_rows - k * nb:hist_rows - k * nb + m_rows, sl]
            cnt = jnp.minimum(pos + 1, w).astype(F32)
            pooled = (tot / cnt - cur).astype(BF16)
            osl = slice(gi * POOL_OUT_DIM, (gi + 1) * POOL_OUT_DIM)
            y_pool = jnp.dot(pooled, pool_w_ref[gi], preferred_element_type=F32) * pool_scale_ref[:, osl]
            c = D_MODEL // GATE_CHUNK + gi
            vg_ref[gate_rows, c * GATE_CHUNK:(c + 1) * GATE_CHUNK] = gate(c) * y_pool
        return run

    assert POOL_OUT_DIM == GATE_CHUNK
    tasks = []
    for i in range(D_MODEL // GATE_CHUNK):
        tasks += [ssm_gate_task(i), pool_task(i, POOL_WINDOWS[i])]
    n_slots = 2 * tc
    task_at = {(i * n_slots) // len(tasks): t for i, t in enumerate(tasks)}

    slot = 0
    for half in range(2):
        offs = [(k * 2 * BLOCK_STATE, k * 2 * BLOCK_STATE + BLOCK_STATE) for k in range(2)]
        st0 = half * half_cols
        a_re = [jnp.broadcast_to(a_ref[:, st0 + r0:st0 + r0 + BLOCK_STATE], (nb, BLOCK_STATE)) for r0, _ in offs]
        a_im = [jnp.broadcast_to(a_ref[:, st0 + i0:st0 + i0 + BLOCK_STATE], (nb, BLOCK_STATE)) for _, i0 in offs]
        s_re = [st_ref[:, st0 + r0:st0 + r0 + BLOCK_STATE] for r0, _ in offs]
        s_im = [st_ref[:, st0 + i0:st0 + i0 + BLOCK_STATE] for _, i0 in offs]
        for t in range(tc):
            if slot in task_at:
                task_at[slot]()
            slot += 1
            row0 = half * m_rows + t * nb
            rows = slice(row0, row0 + nb)
            for k, (r0, i0) in enumerate(offs):
                n_re = a_re[k] * s_re[k] - a_im[k] * s_im[k] + vg_ref[rows, r0:r0 + BLOCK_STATE]
                n_im = a_re[k] * s_im[k] + a_im[k] * s_re[k] + vg_ref[rows, i0:i0 + BLOCK_STATE]
                vg_ref[rows, r0:r0 + BLOCK_STATE] = n_re
                vg_ref[rows, i0:i0 + BLOCK_STATE] = n_im
                s_re[k], s_im[k] = n_re, n_im
        for k, (r0, i0) in enumerate(offs):
            st_ref[:, st0 + r0:st0 + r0 + BLOCK_STATE] = s_re[k]
            st_ref[:, st0 + i0:st0 + i0 + BLOCK_STATE] = s_im[k]
    s_out_ref[...] = st_ref[...]

    sub = m_rows // MIX_SUBTILES
    for s in range(MIX_SUBTILES):
        rs = slice(s * sub, (s + 1) * sub)

        def rows_of(block_rows):
            return slice(block_rows.start + s * sub, block_rows.start + (s + 1) * sub)

        y_parts = []
        for m in range(SSM_BLOCKS):
            v_rows, v_cols = v_block(m)
            y_parts.append(jnp.dot(vg_ref[rows_of(v_rows), v_cols].astype(BF16), cm_ref[m],
                                   preferred_element_type=F32))
        y = jnp.concatenate(y_parts, axis=-1) + dskip_ref[...] * u_ssm[rs]
        y = jax.nn.gelu(y, approximate=True).astype(BF16)
        glu = jnp.dot(y, w_glu_ref[...], preferred_element_type=F32) + b_glu_ref[...]
        y_ssm = glu[:, 0:D_MODEL] * jax.nn.sigmoid(glu[:, D_MODEL:2 * D_MODEL])

        g_rows = rows_of(gate_rows)
        merged = (vg_ref[g_rows, 0:D_MODEL] * y_ssm + vg_ref[g_rows, D_MODEL:2 * D_MODEL]).astype(BF16)
        mixed = jnp.dot(merged, w_out_ref[...], preferred_element_type=F32)
        o = h[rs] + _rms(mixed, g_post_ref[...])
        for j in range(N_SLABS):
            o_ref[j, rs, :] = o[:, j * LANES:(j + 1) * LANES]

    tail = ext_ref[m_rows:m_rows + hist_rows, :]
    hist_out_ref[...] = tail
    ext_ref[0:hist_rows, :] = tail


def _mix(h, s0, hist0, p, *, pos0, nb, tc, name):
    n = h.shape[1]
    m_rows = nb * tc
    hist_rows = POOL_MAXW * nb
    kern = functools.partial(_mix_kernel, pos0, nb, tc)
    slabs = pl.BlockSpec((N_SLABS, m_rows, LANES), lambda i: (0, i, 0))
    return pl.pallas_call(
        kern,
        grid=(n // m_rows,),
        in_specs=[
            slabs,
            _const_spec((1, D_MODEL)),
            _const_spec((D_MODEL, 3 * D_MODEL)),
            _const_spec((1, 3 * D_MODEL)),
            _const_spec((nb, STATE_COLS)),
            _const_spec((hist_rows, D_POOL)),
            _const_spec((1, STATE_COLS)),
            _const_spec((SSM_BLOCKS, BLOCK_CH, 2 * BLOCK_STATE)),
            _const_spec((SSM_BLOCKS, 2 * BLOCK_STATE, BLOCK_CH)),
            _const_spec((1, D_SSM)),
            _const_spec((D_SSM, 2 * D_MODEL)),
            _const_spec((1, 2 * D_MODEL)),
            _const_spec((len(POOL_WINDOWS), POOL_GROUP_DIM, POOL_OUT_DIM)),
            _const_spec((1, D_MODEL)),
            _const_spec((D_MODEL, D_MODEL)),
            _const_spec((1, D_MODEL)),
        ],
        out_specs=[
            slabs,
            pl.BlockSpec((nb, STATE_COLS), lambda i: (0, 0)),
            pl.BlockSpec((hist_rows, D_POOL), lambda i: (0, 0)),
        ],
        out_shape=[
            jax.ShapeDtypeStruct((N_SLABS, n, LANES), F32),
            jax.ShapeDtypeStruct((nb, STATE_COLS), F32),
            jax.ShapeDtypeStruct((hist_rows, D_POOL), F32),
        ],
        scratch_shapes=[
            pltpu.VMEM((3 * m_rows, STATE_COLS // 2), F32),
            pltpu.VMEM((hist_rows + m_rows, D_POOL), F32),
            pltpu.VMEM((nb, STATE_COLS), F32),
        ],
        compiler_params=pltpu.CompilerParams(
            dimension_semantics=("arbitrary",), vmem_limit_bytes=VMEM_LIMIT_BYTES),
        name=name,
    )(h, p["mix_pre_g"], p["w_in"], p["b_in"], s0, hist0, p["a"], p["bm"], p["cm"], p["ssm_d"],
      p["ssm_w_glu"], p["ssm_b_glu"], p["pool_w"], p["pool_scale"], p["w_out"], p["mix_post_g"])


def _block_diag(x):
    nblk, g, r, c = x.shape
    eye = jnp.eye(g, dtype=x.dtype)
    return jnp.einsum("mgrc,gh->mgrhc", x, eye).reshape(nblk, g * r, g * c)


def _ssm_params(lam_re, lam_im, log_step, b_re, b_im, c_re, c_im):
    dt = jnp.exp(log_step)[:, None]
    mag = jnp.exp(lam_re * dt)
    ab_re = mag * jnp.cos(lam_im * dt)
    ab_im = mag * jnp.sin(lam_im * dt)
    den = lam_re * lam_re + lam_im * lam_im
    nr = ab_re - 1.0
    f_re = (nr * lam_re + ab_im * lam_im) / den
    f_im = (ab_im * lam_re - nr * lam_im) / den
    bb_re = f_re[..., None] * b_re - f_im[..., None] * b_im
    bb_im = f_re[..., None] * b_im + f_im[..., None] * b_re

    def blk(x):
        return x.reshape((SSM_BLOCKS, GROUPS_PER_BLOCK) + x.shape[1:])

    bm = jnp.concatenate([
        _block_diag(jnp.swapaxes(blk(bb_re), 2, 3)),
        _block_diag(jnp.swapaxes(blk(bb_im), 2, 3)),
    ], axis=-1).astype(BF16)
    cm = jnp.concatenate([
        _block_diag(jnp.swapaxes(blk(c_re), 2, 3)),
        _block_diag(jnp.swapaxes(blk(-c_im), 2, 3)),
    ], axis=-2).astype(BF16)
    a = jnp.concatenate([blk(ab_re).reshape(SSM_BLOCKS, BLOCK_STATE),
                         blk(ab_im).reshape(SSM_BLOCKS, BLOCK_STATE)], axis=-1).reshape(1, STATE_COLS)
    return a, bm, cm


def _state_to_cols(s_re, s_im):
    nb = s_re.shape[0]
    re = s_re.reshape(nb, SSM_BLOCKS, BLOCK_STATE)
    im = s_im.reshape(nb, SSM_BLOCKS, BLOCK_STATE)
    return jnp.stack([re, im], axis=2).reshape(nb, STATE_COLS)


def _cols_to_state(s):
    nb = s.shape[0]
    s = s.reshape(nb, SSM_BLOCKS, 2, GROUPS_PER_BLOCK, SSM_STATE)
    re = s[:, :, 0].reshape(nb, SSM_GROUPS, SSM_STATE)
    im = s[:, :, 1].reshape(nb, SSM_GROUPS, SSM_STATE)
    return re, im


def _layer(x, s0, hist0, p, *, pos0, tt, tc):
    nb, t_len, _ = x.shape
    h = _ffn(x, p["ffn1_pre_g"], p["ffn1_w_in"], p["ffn1_w_out"], p["ffn1_half_post_g"],
             nb=nb, t_len=t_len, tt=tt, rows_in=True, name="ffn1")
    h, s_fin, hist = _mix(h, s0, hist0, p, pos0=pos0, nb=nb, tc=tc, name="mix")
    y = _ffn(h, p["ffn2_pre_g"], p["ffn2_w_in"], p["ffn2_w_out"], p["ffn2_half_post_g"],
             nb=nb, t_len=t_len, tt=tt, rows_in=False, name="ffn2")
    s_re, s_im = _cols_to_state(s_fin)
    pool = jnp.swapaxes(hist.reshape(POOL_MAXW, nb, D_POOL)[1:], 0, 1)
    return y, s_re, s_im, pool


def kernel(x_prompt, x_sample, state_ssm_re, state_ssm_im, cache_pool, ffn1_pre_g, ffn1_w_in, ffn1_w_out, ffn1_post_g, mix_pre_g, w_in, b_in, ssm_lambda_re, ssm_lambda_im, ssm_log_step, ssm_b_re, ssm_b_im, ssm_c_re, ssm_c_im, ssm_d, ssm_w_glu, ssm_b_glu, pool_w, pool_scale, w_out, mix_post_g, ffn2_pre_g, ffn2_w_in, ffn2_w_out, ffn2_post_g):
    depth = ffn1_w_in.shape[0]
    nb_p = x_prompt.shape[0]
    nb_s, t_s, _ = x_sample.shape

    hp, hs = x_prompt, x_sample
    outs = [[] for _ in range(6)]
    for l in range(depth):
        a, bm, cm = _ssm_params(ssm_lambda_re[l], ssm_lambda_im[l], ssm_log_step[l],
                                ssm_b_re[l], ssm_b_im[l], ssm_c_re[l], ssm_c_im[l])
        p = {
            "ffn1_pre_g": ffn1_pre_g[l][None], "ffn1_w_in": ffn1_w_in[l].astype(BF16),
            "ffn1_w_out": ffn1_w_out[l].astype(BF16), "ffn1_half_post_g": 0.5 * ffn1_post_g[l][None],
            "mix_pre_g": mix_pre_g[l][None], "w_in": w_in[l].astype(BF16), "b_in": b_in[l][None],
            "a": a, "bm": bm, "cm": cm, "ssm_d": ssm_d[l][None],
            "ssm_w_glu": ssm_w_glu[l].astype(BF16), "ssm_b_glu": ssm_b_glu[l][None],
            "pool_w": pool_w[l].astype(BF16), "pool_scale": pool_scale[l][None],
            "w_out": w_out[l].astype(BF16), "mix_post_g": mix_post_g[l][None],
            "ffn2_pre_g": ffn2_pre_g[l][None], "ffn2_w_in": ffn2_w_in[l].astype(BF16),
            "ffn2_w_out": ffn2_w_out[l].astype(BF16), "ffn2_half_post_g": 0.5 * ffn2_post_g[l][None],
        }
        s0 = jnp.zeros((nb_p, STATE_COLS), F32)
        hist0 = jnp.zeros((POOL_MAXW * nb_p, D_POOL), F32)
        hp, s_re, s_im, pool = _layer(hp, s0, hist0, p, pos0=0, tt=128, tc=64)
        for lst, val in zip(outs[:3], (s_re, s_im, pool)):
            lst.append(val)
        s0 = _state_to_cols(state_ssm_re[l], state_ssm_im[l])
        hist0 = jnp.pad(jnp.swapaxes(cache_pool[l], 0, 1), ((1, 0), (0, 0), (0, 0))).reshape(
            POOL_MAXW * nb_s, D_POOL)
        hs, s_re, s_im, pool = _layer(hs, s0, hist0, p, pos0=PAST_LEN, tt=t_s, tc=t_s)
        for lst, val in zip(outs[3:], (s_re, s_im, pool)):
            lst.append(val)
    return (hp, hs) + tuple(jnp.stack(o) for o in outs)
```

```python
import functools

import jax
import jax.numpy as jnp
from jax import lax
from jax.experimental import pallas as pl
from jax.experimental.pallas import tpu as pltpu

F32 = jnp.float32
BF16 = jnp.bfloat16

LANES = 128
D_MODEL = 1024
D_FF = 2816
D_SSM = 512
D_POOL = 512
SSM_GROUP_DIM = 16
SSM_GROUPS = 32
SSM_STATE = 64
POOL_WINDOWS = (2, 4, 8, 16)
POOL_GROUP_DIM = 128
POOL_OUT_DIM = 256
POOL_MAXW = 16
PAST_LEN = 2048
EPS = 1e-6

N_SLABS = D_MODEL // LANES

SSM_BLOCKS = 4
GROUPS_PER_BLOCK = SSM_GROUPS // SSM_BLOCKS
BLOCK_CH = GROUPS_PER_BLOCK * SSM_GROUP_DIM
BLOCK_STATE = GROUPS_PER_BLOCK * SSM_STATE
STATE_COLS = 2 * SSM_GROUPS * SSM_STATE

FFN_CHUNK = 256
FFN_SUBTILES = 2
MIX_SEGMENTS = 2
GATE_CHUNK = 256
VMEM_LIMIT_BYTES = 56 * 1024 * 1024


def _rms(x, g):
    return x * lax.rsqrt(jnp.mean(x * x, axis=-1, keepdims=True) + EPS) * g


def _const_spec(shape):
    zeros = (0,) * len(shape)
    return pl.BlockSpec(shape, lambda i: zeros, pipeline_mode=pl.Buffered(1))


def _ffn_body(x, pre_g_ref, w_in_ref, w_out_ref, half_post_g_ref, xn_ref, g_ref):
    rows = x.shape[0]
    sub = rows // FFN_SUBTILES
    outs = []
    for s in range(FFN_SUBTILES):
        r = slice(s * sub, (s + 1) * sub)
        xs = x[r]
        xn_ref[r, :] = _rms(xs, pre_g_ref[...]).astype(BF16)
        for j in range(D_FF // FFN_CHUNK):
            lo, hi = j * FFN_CHUNK, D_FF + j * FFN_CHUNK
            xn = xn_ref[r, :]
            a = jnp.dot(xn, w_in_ref[:, lo:lo + FFN_CHUNK], preferred_element_type=F32)
            b = jnp.dot(xn, w_in_ref[:, hi:hi + FFN_CHUNK], preferred_element_type=F32)
            g_ref[r, lo:lo + FFN_CHUNK] = (a * jax.nn.sigmoid(a) * b).astype(BF16)
        y = jnp.dot(g_ref[r, :], w_out_ref[...], preferred_element_type=F32)
        outs.append(xs + _rms(y, half_post_g_ref[...]))
    return jnp.concatenate(outs, axis=0)


def _ffn_in_kernel(nb, tt, x_ref, pre_g_ref, w_in_ref, w_out_ref, half_post_g_ref, o_ref, xn_ref, g_ref):
    x = x_ref[...].reshape(nb * tt, D_MODEL)
    o = _ffn_body(x, pre_g_ref, w_in_ref, w_out_ref, half_post_g_ref, xn_ref, g_ref)
    for j in range(N_SLABS):
        for b in range(nb):
            o_ref[j, pl.ds(b, tt, stride=nb), :] = o[b * tt:(b + 1) * tt, j * LANES:(j + 1) * LANES]


def _ffn_out_kernel(nb, tt, x_ref, pre_g_ref, w_in_ref, w_out_ref, half_post_g_ref, o_ref, xn_ref, g_ref):
    x = jnp.concatenate([
        jnp.concatenate([x_ref[j, pl.ds(b, tt, stride=nb), :] for b in range(nb)], axis=0)
        for j in range(N_SLABS)], axis=1)
    o = _ffn_body(x, pre_g_ref, w_in_ref, w_out_ref, half_post_g_ref, xn_ref, g_ref)
    o_ref[...] = o.reshape(nb, tt, D_MODEL)


def _ffn(x, pre_g, w_in, w_out, half_post_g, *, nb, t_len, tt, rows_in, name):
    tm = nb * tt
    natural = pl.BlockSpec((nb, tt, D_MODEL), lambda i: (0, i, 0))
    slabs = pl.BlockSpec((N_SLABS, tm, LANES), lambda i: (0, i, 0))
    natural_shape = jax.ShapeDtypeStruct((nb, t_len, D_MODEL), F32)
    slabs_shape = jax.ShapeDtypeStruct((N_SLABS, t_len * nb, LANES), F32)
    kern = functools.partial(_ffn_in_kernel if rows_in else _ffn_out_kernel, nb, tt)
    return pl.pallas_call(
        kern,
        grid=(t_len // tt,),
        in_specs=[
            natural if rows_in else slabs,
            _const_spec((1, D_MODEL)),
            _const_spec((D_MODEL, 2 * D_FF)),
            _const_spec((D_FF, D_MODEL)),
            _const_spec((1, D_MODEL)),
        ],
        out_specs=slabs if rows_in else natural,
        out_shape=slabs_shape if rows_in else natural_shape,
        scratch_shapes=[
            pltpu.VMEM((tm, D_MODEL), BF16),
            pltpu.VMEM((tm, D_FF), BF16),
        ],
        compiler_params=pltpu.CompilerParams(
            dimension_semantics=("arbitrary",), vmem_limit_bytes=VMEM_LIMIT_BYTES),
        name=name,
    )(x, pre_g, w_in, w_out, half_post_g)


def _mix_kernel(pos0, nb, tc,
                h_ref, g_pre_ref, w_in_ref, b_in_ref, s0_ref, hist0_ref, a_ref, bm_ref, cm_ref,
                dskip_ref, w_glu_ref, b_glu_ref, pool_w_ref, pool_scale_ref, w_out_ref, g_post_ref,
                o_ref, s_out_ref, hist_out_ref,
                vg_ref, ext_ref, st_ref, inv_cnt_ref):
    m_rows = nb * tc
    hist_rows = POOL_MAXW * nb
    step = pl.program_id(0)
    half_cols = 2 * 2 * BLOCK_STATE
    gate_rows = slice(2 * m_rows, 3 * m_rows)

    @pl.when(step == 0)
    def _():
        st_ref[...] = s0_ref[...]
        ext_ref[0:hist_rows, :] = hist0_ref[...]

    assert tc >= POOL_MAXW - 1
    rows_per_iter = LANES // 2

    def fill_inv_cnt(first_pos):
        def body(i, carry):
            r0 = pl.multiple_of(i * rows_per_iter, rows_per_iter)
            row_id = r0 + lax.broadcasted_iota(jnp.int32, (rows_per_iter, POOL_GROUP_DIM), 0)
            pos = first_pos + row_id // nb
            for gi, w in enumerate(POOL_WINDOWS):
                inv_cnt_ref[pl.ds(gi * m_rows + r0, rows_per_iter), :] = 1.0 / jnp.minimum(pos + 1, w).astype(F32)
            return carry
        lax.fori_loop(0, m_rows // rows_per_iter, body, 0)

    @pl.when(step == 0)
    def _():
        fill_inv_cnt(pos0)

    @pl.when(step == 1)
    def _():
        fill_inv_cnt(pos0 + tc)

    sub = m_rows // MIX_SEGMENTS
    seg_steps = tc // MIX_SEGMENTS

    def v_block(m, s):
        half, k = divmod(m, 2)
        return (slice(half * m_rows + s * sub, half * m_rows + (s + 1) * sub),
                slice(k * 2 * BLOCK_STATE, (k + 1) * 2 * BLOCK_STATE))

    h_seg, z_seg, u_ssm_seg = [], [], []
    for s in range(MIX_SEGMENTS):
        rs = slice(s * sub, (s + 1) * sub)
        h_s = jnp.concatenate([h_ref[j, rs, :] for j in range(N_SLABS)], axis=1)
        z_s = _rms(h_s, g_pre_ref[...]).astype(BF16)
        u_s = jnp.dot(z_s, w_in_ref[:, 0:D_SSM + D_POOL], preferred_element_type=F32) + b_in_ref[:, 0:D_SSM + D_POOL]
        ext_ref[hist_rows + s * sub:hist_rows + (s + 1) * sub, :] = u_s[:, D_SSM:D_SSM + D_POOL]
        u_ssm_bf = u_s[:, 0:D_SSM].astype(BF16)
        for m in range(SSM_BLOCKS):
            vg_ref[v_block(m, s)] = jnp.dot(
                u_ssm_bf[:, m * BLOCK_CH:(m + 1) * BLOCK_CH], bm_ref[m], preferred_element_type=F32)
        h_seg.append(h_s)
        z_seg.append(z_s)
        u_ssm_seg.append(u_s[:, 0:D_SSM])
    z = jnp.concatenate(z_seg, axis=0)

    def gate(c):
        lo = D_SSM + D_POOL + c * GATE_CHUNK
        gl = jnp.dot(z, w_in_ref[:, lo:lo + GATE_CHUNK], preferred_element_type=F32) + b_in_ref[:, lo:lo + GATE_CHUNK]
        return jax.nn.sigmoid(gl)

    def ssm_gate_task(c):
        def run():
            vg_ref[gate_rows, c * GATE_CHUNK:(c + 1) * GATE_CHUNK] = gate(c)
        return run

    def pool_task(gi, w):
        def run():
            sl = slice(gi * POOL_GROUP_DIM, (gi + 1) * POOL_GROUP_DIM)
            cur = ext_ref[hist_rows:hist_rows + m_rows, sl]
            tot = cur
            for k in range(1, w):
                tot = tot + ext_ref[hist_rows - k * nb:hist_rows - k * nb + m_rows, sl]
            pooled = (tot * inv_cnt_ref[gi * m_rows:(gi + 1) * m_rows, :] - cur).astype(BF16)
            osl = slice(gi * POOL_OUT_DIM, (gi + 1) * POOL_OUT_DIM)
            y_pool = jnp.dot(pooled, pool_w_ref[gi], preferred_element_type=F32) * pool_scale_ref[:, osl]
            c = D_MODEL // GATE_CHUNK + gi
            vg_ref[gate_rows, c * GATE_CHUNK:(c + 1) * GATE_CHUNK] = gate(c) * y_pool
        return run

    def post_scan(s):
        rs = slice(s * sub, (s + 1) * sub)
        y_parts = [
            jnp.dot(vg_ref[v_block(m, s)].astype(BF16), cm_ref[m], preferred_element_type=F32)
            for m in range(SSM_BLOCKS)
        ]
        y = jnp.concatenate(y_parts, axis=-1) + dskip_ref[...] * u_ssm_seg[s]
        y = jax.nn.gelu(y, approximate=True).astype(BF16)
        glu = jnp.dot(y, w_glu_ref[...], preferred_element_type=F32) + b_glu_ref[...]
        y_ssm = glu[:, 0:D_MODEL] * jax.nn.sigmoid(glu[:, D_MODEL:2 * D_MODEL])
        g_rows = slice(gate_rows.start + s * sub, gate_rows.start + (s + 1) * sub)
        merged = (vg_ref[g_rows, 0:D_MODEL] * y_ssm + vg_ref[g_rows, D_MODEL:2 * D_MODEL]).astype(BF16)
        mixed = jnp.dot(merged, w_out_ref[...], preferred_element_type=F32)
        o = h_seg[s] + _rms(mixed, g_post_ref[...])
        for j in range(N_SLABS):
            o_ref[j, rs, :] = o[:, j * LANES:(j + 1) * LANES]

    assert POOL_OUT_DIM == GATE_CHUNK
    tasks = []
    for i in range(D_MODEL // GATE_CHUNK):
        tasks += [ssm_gate_task(i), pool_task(i, POOL_WINDOWS[i])]
    n_slots = 2 * seg_steps
    task_at = {(i * n_slots) // len(tasks): t for i, t in enumerate(tasks)}

    slot = 0
    offs = [(k * 2 * BLOCK_STATE, k * 2 * BLOCK_STATE + BLOCK_STATE) for k in range(2)]
    for seg in range(MIX_SEGMENTS):
        if seg > 0:
            post_scan(seg - 1)
        for half in range(2):
            st0 = half * half_cols
            a_re = [jnp.broadcast_to(a_ref[:, st0 + r0:st0 + r0 + BLOCK_STATE], (nb, BLOCK_STATE)) for r0, _ in offs]
            a_im = [jnp.broadcast_to(a_ref[:, st0 + i0:st0 + i0 + BLOCK_STATE], (nb, BLOCK_STATE)) for _, i0 in offs]
            s_re = [st_ref[:, st0 + r0:st0 + r0 + BLOCK_STATE] for r0, _ in offs]
            s_im = [st_ref[:, st0 + i0:st0 + i0 + BLOCK_STATE] for _, i0 in offs]
            for t in range(seg * seg_steps, (seg + 1) * seg_steps):
                if slot in task_at:
                    task_at[slot]()
                slot += 1
                row0 = half * m_rows + t * nb
                rows = slice(row0, row0 + nb)
                for k, (r0, i0) in enumerate(offs):
                    n_re = a_re[k] * s_re[k] - a_im[k] * s_im[k] + vg_ref[rows, r0:r0 + BLOCK_STATE]
                    n_im = a_re[k] * s_im[k] + a_im[k] * s_re[k] + vg_ref[rows, i0:i0 + BLOCK_STATE]
                    vg_ref[rows, r0:r0 + BLOCK_STATE] = n_re
                    vg_ref[rows, i0:i0 + BLOCK_STATE] = n_im
                    s_re[k], s_im[k] = n_re, n_im
            for k, (r0, i0) in enumerate(offs):
                st_ref[:, st0 + r0:st0 + r0 + BLOCK_STATE] = s_re[k]
                st_ref[:, st0 + i0:st0 + i0 + BLOCK_STATE] = s_im[k]
    s_out_ref[...] = st_ref[...]
    post_scan(MIX_SEGMENTS - 1)

    tail = ext_ref[m_rows:m_rows + hist_rows, :]
    hist_out_ref[...] = tail
    ext_ref[0:hist_rows, :] = tail


def _mix(h, s0, hist0, p, *, pos0, nb, tc, name):
    n = h.shape[1]
    m_rows = nb * tc
    hist_rows = POOL_MAXW * nb
    kern = functools.partial(_mix_kernel, pos0, nb, tc)
    slabs = pl.BlockSpec((N_SLABS, m_rows, LANES), lambda i: (0, i, 0))
    return pl.pallas_call(
        kern,
        grid=(n // m_rows,),
        in_specs=[
            slabs,
            _const_spec((1, D_MODEL)),
            _const_spec((D_MODEL, 3 * D_MODEL)),
            _const_spec((1, 3 * D_MODEL)),
            _const_spec((nb, STATE_COLS)),
            _const_spec((hist_rows, D_POOL)),
            _const_spec((1, STATE_COLS)),
            _const_spec((SSM_BLOCKS, BLOCK_CH, 2 * BLOCK_STATE)),
            _const_spec((SSM_BLOCKS, 2 * BLOCK_STATE, BLOCK_CH)),
            _const_spec((1, D_SSM)),
            _const_spec((D_SSM, 2 * D_MODEL)),
            _const_spec((1, 2 * D_MODEL)),
            _const_spec((len(POOL_WINDOWS), POOL_GROUP_DIM, POOL_OUT_DIM)),
            _const_spec((1, D_MODEL)),
            _const_spec((D_MODEL, D_MODEL)),
            _const_spec((1, D_MODEL)),
        ],
        out_specs=[
            slabs,
            pl.BlockSpec((nb, STATE_COLS), lambda i: (0, 0)),
            pl.BlockSpec((hist_rows, D_POOL), lambda i: (0, 0)),
        ],
        out_shape=[
            jax.ShapeDtypeStruct((N_SLABS, n, LANES), F32),
            jax.ShapeDtypeStruct((nb, STATE_COLS), F32),
            jax.ShapeDtypeStruct((hist_rows, D_POOL), F32),
        ],
        scratch_shapes=[
            pltpu.VMEM((3 * m_rows, STATE_COLS // 2), F32),
            pltpu.VMEM((hist_rows + m_rows, D_POOL), F32),
            pltpu.VMEM((nb, STATE_COLS), F32),
            pltpu.VMEM((len(POOL_WINDOWS) * m_rows, POOL_GROUP_DIM), F32),
        ],
        compiler_params=pltpu.CompilerParams(
            dimension_semantics=("arbitrary",), vmem_limit_bytes=VMEM_LIMIT_BYTES),
        name=name,
    )(h, p["mix_pre_g"], p["w_in"], p["b_in"], s0, hist0, p["a"], p["bm"], p["cm"], p["ssm_d"],
      p["ssm_w_glu"], p["ssm_b_glu"], p["pool_w"], p["pool_scale"], p["w_out"], p["mix_post_g"])


def _block_diag(x):
    nblk, g, r, c = x.shape
    eye = jnp.eye(g, dtype=x.dtype)
    return jnp.einsum("mgrc,gh->mgrhc", x, eye).reshape(nblk, g * r, g * c)


def _ssm_params(lam_re, lam_im, log_step, b_re, b_im, c_re, c_im):
    dt = jnp.exp(log_step)[:, None]
    mag = jnp.exp(lam_re * dt)
    ab_re = mag * jnp.cos(lam_im * dt)
    ab_im = mag * jnp.sin(lam_im * dt)
    den = lam_re * lam_re + lam_im * lam_im
    nr = ab_re - 1.0
    f_re = (nr * lam_re + ab_im * lam_im) / den
    f_im = (ab_im * lam_re - nr * lam_im) / den
    bb_re = f_re[..., None] * b_re - f_im[..., None] * b_im
    bb_im = f_re[..., None] * b_im + f_im[..., None] * b_re

    def blk(x):
        return x.reshape((SSM_BLOCKS, GROUPS_PER_BLOCK) + x.shape[1:])

    bm = jnp.concatenate([
        _block_diag(jnp.swapaxes(blk(bb_re), 2, 3)),
        _block_diag(jnp.swapaxes(blk(bb_im), 2, 3)),
    ], axis=-1).astype(BF16)
    cm = jnp.concatenate([
        _block_diag(jnp.swapaxes(blk(c_re), 2, 3)),
        _block_diag(jnp.swapaxes(blk(-c_im), 2, 3)),
    ], axis=-2).astype(BF16)
    a = jnp.concatenate([blk(ab_re).reshape(SSM_BLOCKS, BLOCK_STATE),
                         blk(ab_im).reshape(SSM_BLOCKS, BLOCK_STATE)], axis=-1).reshape(1, STATE_COLS)
    return a, bm, cm


def _state_to_cols(s_re, s_im):
    nb = s_re.shape[0]
    re = s_re.reshape(nb, SSM_BLOCKS, BLOCK_STATE)
    im = s_im.reshape(nb, SSM_BLOCKS, BLOCK_STATE)
    return jnp.stack([re, im], axis=2).reshape(nb, STATE_COLS)


def _cols_to_state(s):
    nb = s.shape[0]
    s = s.reshape(nb, SSM_BLOCKS, 2, GROUPS_PER_BLOCK, SSM_STATE)
    re = s[:, :, 0].reshape(nb, SSM_GROUPS, SSM_STATE)
    im = s[:, :, 1].reshape(nb, SSM_GROUPS, SSM_STATE)
    return re, im


def _layer(x, s0, hist0, p, *, pos0, tt, tc):
    nb, t_len, _ = x.shape
    h = _ffn(x, p["ffn1_pre_g"], p["ffn1_w_in"], p["ffn1_w_out"], p["ffn1_half_post_g"],
             nb=nb, t_len=t_len, tt=tt, rows_in=True, name="ffn1")
    h, s_fin, hist = _mix(h, s0, hist0, p, pos0=pos0, nb=nb, tc=tc, name="mix")
    y = _ffn(h, p["ffn2_pre_g"], p["ffn2_w_in"], p["ffn2_w_out"], p["ffn2_half_post_g"],
             nb=nb, t_len=t_len, tt=tt, rows_in=False, name="ffn2")
    s_re, s_im = _cols_to_state(s_fin)
    pool = jnp.swapaxes(hist.reshape(POOL_MAXW, nb, D_POOL)[1:], 0, 1)
    return y, s_re, s_im, pool


def kernel(x_prompt, x_sample, state_ssm_re, state_ssm_im, cache_pool, ffn1_pre_g, ffn1_w_in, ffn1_w_out, ffn1_post_g, mix_pre_g, w_in, b_in, ssm_lambda_re, ssm_lambda_im, ssm_log_step, ssm_b_re, ssm_b_im, ssm_c_re, ssm_c_im, ssm_d, ssm_w_glu, ssm_b_glu, pool_w, pool_scale, w_out, mix_post_g, ffn2_pre_g, ffn2_w_in, ffn2_w_out, ffn2_post_g):
    depth = ffn1_w_in.shape[0]
    nb_p = x_prompt.shape[0]
    nb_s, t_s, _ = x_sample.shape

    hp, hs = x_prompt, x_sample
    outs = [[] for _ in range(6)]
    for l in range(depth):
        a, bm, cm = _ssm_params(ssm_lambda_re[l], ssm_lambda_im[l], ssm_log_step[l],
                                ssm_b_re[l], ssm_b_im[l], ssm_c_re[l], ssm_c_im[l])
        p = {
            "ffn1_pre_g": ffn1_pre_g[l][None], "ffn1_w_in": ffn1_w_in[l].astype(BF16),
            "ffn1_w_out": ffn1_w_out[l].astype(BF16), "ffn1_half_post_g": 0.5 * ffn1_post_g[l][None],
            "mix_pre_g": mix_pre_g[l][None], "w_in": w_in[l].astype(BF16), "b_in": b_in[l][None],
            "a": a, "bm": bm, "cm": cm, "ssm_d": ssm_d[l][None],
            "ssm_w_glu": ssm_w_glu[l].astype(BF16), "ssm_b_glu": ssm_b_glu[l][None],
            "pool_w": pool_w[l].astype(BF16), "pool_scale": pool_scale[l][None],
            "w_out": w_out[l].astype(BF16), "mix_post_g": mix_post_g[l][None],
            "ffn2_pre_g": ffn2_pre_g[l][None], "ffn2_w_in": ffn2_w_in[l].astype(BF16),
            "ffn2_w_out": ffn2_w_out[l].astype(BF16), "ffn2_half_post_g": 0.5 * ffn2_post_g[l][None],
        }
        s0 = jnp.zeros((nb_p, STATE_COLS), F32)
        hist0 = jnp.zeros((POOL_MAXW * nb_p, D_POOL), F32)
        hp, s_re, s_im, pool = _layer(hp, s0, hist0, p, pos0=0, tt=128, tc=64)
        for lst, val in zip(outs[:3], (s_re, s_im, pool)):
            lst.append(val)
        s0 = _state_to_cols(state_ssm_re[l], state_ssm_im[l])
        hist0 = jnp.pad(jnp.swapaxes(cache_pool[l], 0, 1), ((1, 0), (0, 0), (0, 0))).reshape(
            POOL_MAXW * nb_s, D_POOL)
        hs, s_re, s_im, pool = _layer(hs, s0, hist0, p, pos0=PAST_LEN, tt=t_s, tc=t_s)
        for lst, val in zip(outs[3:], (s_re, s_im, pool)):
            lst.append(val)
    return (hp, hs) + tuple(jnp.stack(o) for o in outs)
```

```python
import functools

import jax
import jax.numpy as jnp
from jax import lax
from jax.experimental import pallas as pl
from jax.experimental.pallas import tpu as pltpu

F32 = jnp.float32
BF16 = jnp.bfloat16

LANES = 128
D_MODEL = 1024
D_FF = 2816
D_SSM = 512
D_POOL = 512
SSM_GROUP_DIM = 16
SSM_GROUPS = 32
SSM_STATE = 64
POOL_WINDOWS = (2, 4, 8, 16)
POOL_GROUP_DIM = 128
POOL_OUT_DIM = 256
POOL_MAXW = 16
PAST_LEN = 2048
EPS = 1e-6

N_SLABS = D_MODEL // LANES

SSM_BLOCKS = 4
GROUPS_PER_BLOCK = SSM_GROUPS // SSM_BLOCKS
BLOCK_CH = GROUPS_PER_BLOCK * SSM_GROUP_DIM
BLOCK_STATE = GROUPS_PER_BLOCK * SSM_STATE
STATE_COLS = 2 * SSM_GROUPS * SSM_STATE

FFN_CHUNK = 256
FFN_SUBTILES = 2
MIX_SUBTILES = 2
GATE_CHUNK = 256
VMEM_LIMIT_BYTES = 56 * 1024 * 1024


def _rms(x, g):
    return x * lax.rsqrt(jnp.mean(x * x, axis=-1, keepdims=True) + EPS) * g


def _const_spec(shape):
    zeros = (0,) * len(shape)
    return pl.BlockSpec(shape, lambda i: zeros, pipeline_mode=pl.Buffered(1))


def _ffn_body(x, pre_g_ref, w_in_ref, w_out_ref, half_post_g_ref, xn_ref, g_ref):
    rows = x.shape[0]
    sub = rows // FFN_SUBTILES
    outs = []
    for s in range(FFN_SUBTILES):
        r = slice(s * sub, (s + 1) * sub)
        xs = x[r]
        xn_ref[r, :] = _rms(xs, pre_g_ref[...]).astype(BF16)
        for j in range(D_FF // FFN_CHUNK):
            lo, hi = j * FFN_CHUNK, D_FF + j * FFN_CHUNK
            xn = xn_ref[r, :]
            a = jnp.dot(xn, w_in_ref[:, lo:lo + FFN_CHUNK], preferred_element_type=F32)
            b = jnp.dot(xn, w_in_ref[:, hi:hi + FFN_CHUNK], preferred_element_type=F32)
            g_ref[r, lo:lo + FFN_CHUNK] = (a * jax.nn.sigmoid(a) * b).astype(BF16)
        y = jnp.dot(g_ref[r, :], w_out_ref[...], preferred_element_type=F32)
        outs.append(xs + _rms(y, half_post_g_ref[...]))
    return jnp.concatenate(outs, axis=0)


def _ffn_in_kernel(nb, tt, x_ref, pre_g_ref, w_in_ref, w_out_ref, half_post_g_ref, o_ref, xn_ref, g_ref):
    x = x_ref[...].reshape(nb * tt, D_MODEL)
    o = _ffn_body(x, pre_g_ref, w_in_ref, w_out_ref, half_post_g_ref, xn_ref, g_ref)
    for j in range(N_SLABS):
        for b in range(nb):
            o_ref[j, pl.ds(b, tt, stride=nb), :] = o[b * tt:(b + 1) * tt, j * LANES:(j + 1) * LANES]


def _ffn_out_kernel(nb, tt, x_ref, pre_g_ref, w_in_ref, w_out_ref, half_post_g_ref, o_ref, xn_ref, g_ref):
    x = jnp.concatenate([
        jnp.concatenate([x_ref[j, pl.ds(b, tt, stride=nb), :] for b in range(nb)], axis=0)
        for j in range(N_SLABS)], axis=1)
    o = _ffn_body(x, pre_g_ref, w_in_ref, w_out_ref, half_post_g_ref, xn_ref, g_ref)
    o_ref[...] = o.reshape(nb, tt, D_MODEL)


def _ffn(x, pre_g, w_in, w_out, half_post_g, *, nb, t_len, tt, rows_in, name):
    tm = nb * tt
    natural = pl.BlockSpec((nb, tt, D_MODEL), lambda i: (0, i, 0))
    slabs = pl.BlockSpec((N_SLABS, tm, LANES), lambda i: (0, i, 0))
    natural_shape = jax.ShapeDtypeStruct((nb, t_len, D_MODEL), F32)
    slabs_shape = jax.ShapeDtypeStruct((N_SLABS, t_len * nb, LANES), F32)
    kern = functools.partial(_ffn_in_kernel if rows_in else _ffn_out_kernel, nb, tt)
    return pl.pallas_call(
        kern,
        grid=(t_len // tt,),
        in_specs=[
            natural if rows_in else slabs,
            _const_spec((1, D_MODEL)),
            _const_spec((D_MODEL, 2 * D_FF)),
            _const_spec((D_FF, D_MODEL)),
            _const_spec((1, D_MODEL)),
        ],
        out_specs=slabs if rows_in else natural,
        out_shape=slabs_shape if rows_in else natural_shape,
        scratch_shapes=[
            pltpu.VMEM((tm, D_MODEL), BF16),
            pltpu.VMEM((tm, D_FF), BF16),
        ],
        compiler_params=pltpu.CompilerParams(
            dimension_semantics=("arbitrary",), vmem_limit_bytes=VMEM_LIMIT_BYTES),
        name=name,
    )(x, pre_g, w_in, w_out, half_post_g)


def _mix_kernel(pos0, nb, tc,
                h_ref, g_pre_ref, w_in_ref, b_in_ref, s0_ref, hist0_ref, a2_ref, bpair_ref, ca_ref, t2_ref,
                dskip_ref, w_glu_ref, b_glu_ref, pool_w_ref, pool_scale_ref, w_out_ref, g_post_ref,
                o_ref, s_out_ref, hist_out_ref,
                sp_ref, gate_ref, ext_ref, st_ref, yord_ref):
    m_rows = nb * tc
    n_pairs = tc // 2
    p_rows = nb * n_pairs
    hist_rows = POOL_MAXW * nb
    step = pl.program_id(0)
    half_cols = 2 * 2 * BLOCK_STATE

    @pl.when(step == 0)
    def _():
        st_ref[...] = s0_ref[...]
        ext_ref[0:hist_rows, :] = hist0_ref[...]

    h = jnp.concatenate([h_ref[j] for j in range(N_SLABS)], axis=1)
    z = _rms(h, g_pre_ref[...]).astype(BF16)

    u = jnp.dot(z, w_in_ref[:, 0:D_SSM + D_POOL], preferred_element_type=F32) + b_in_ref[:, 0:D_SSM + D_POOL]
    u_ssm = u[:, 0:D_SSM]
    ext_ref[hist_rows:hist_rows + m_rows, :] = u[:, D_SSM:D_SSM + D_POOL]

    def sp_block(m, first_slab):
        half, k = divmod(m, 2)
        return (half, slice(first_slab * nb, first_slab * nb + p_rows),
                slice(k * 2 * BLOCK_STATE, (k + 1) * 2 * BLOCK_STATE))

    u_pairs = u_ssm.reshape(n_pairs, 2 * nb, D_SSM)
    u_even = u_pairs[:, 0:nb, :].reshape(p_rows, D_SSM).astype(BF16)
    u_odd = u_pairs[:, nb:2 * nb, :].reshape(p_rows, D_SSM).astype(BF16)
    u_pair = [jnp.concatenate([u_even[:, m * BLOCK_CH:(m + 1) * BLOCK_CH],
                               u_odd[:, m * BLOCK_CH:(m + 1) * BLOCK_CH]], axis=1) for m in range(SSM_BLOCKS)]

    for m in range(SSM_BLOCKS):
        sp_ref[sp_block(m, 1)] = jnp.dot(u_pair[m], bpair_ref[m], preferred_element_type=F32)

    def gate(c):
        lo = D_SSM + D_POOL + c * GATE_CHUNK
        gl = jnp.dot(z, w_in_ref[:, lo:lo + GATE_CHUNK], preferred_element_type=F32) + b_in_ref[:, lo:lo + GATE_CHUNK]
        return jax.nn.sigmoid(gl)

    def ssm_gate_task(c):
        def run():
            gate_ref[:, c * GATE_CHUNK:(c + 1) * GATE_CHUNK] = gate(c)
        return run

    def pool_task(gi, w):
        def run():
            sl = slice(gi * POOL_GROUP_DIM, (gi + 1) * POOL_GROUP_DIM)
            row_id = lax.broadcasted_iota(jnp.int32, (m_rows, POOL_GROUP_DIM), 0)
            pos = pos0 + step * tc + row_id // nb
            cur = ext_ref[hist_rows:hist_rows + m_rows, sl]
            tot = cur
            for k in range(1, w):
                tot = tot + ext_ref[hist_rows - k * nb:hist_rows - k * nb + m_rows, sl]
            cnt = jnp.minimum(pos + 1, w).astype(F32)
            pooled = (tot / cnt - cur).astype(BF16)
            osl = slice(gi * POOL_OUT_DIM, (gi + 1) * POOL_OUT_DIM)
            y_pool = jnp.dot(pooled, pool_w_ref[gi], preferred_element_type=F32) * pool_scale_ref[:, osl]
            c = D_MODEL // GATE_CHUNK + gi
            gate_ref[:, c * GATE_CHUNK:(c + 1) * GATE_CHUNK] = gate(c) * y_pool
        return run

    assert POOL_OUT_DIM == GATE_CHUNK
    tasks = []
    for i in range(D_MODEL // GATE_CHUNK):
        tasks += [ssm_gate_task(i), pool_task(i, POOL_WINDOWS[i])]
    n_slots = 2 * n_pairs
    task_at = {(i * n_slots) // len(tasks): t for i, t in enumerate(tasks)}

    slot = 0
    for half in range(2):
        offs = [(k * 2 * BLOCK_STATE, k * 2 * BLOCK_STATE + BLOCK_STATE) for k in range(2)]
        st0 = half * half_cols
        a_re = [jnp.broadcast_to(a2_ref[:, st0 + r0:st0 + r0 + BLOCK_STATE], (nb, BLOCK_STATE)) for r0, _ in offs]
        a_im = [jnp.broadcast_to(a2_ref[:, st0 + i0:st0 + i0 + BLOCK_STATE], (nb, BLOCK_STATE)) for _, i0 in offs]
        s_re = [st_ref[:, st0 + r0:st0 + r0 + BLOCK_STATE] for r0, _ in offs]
        s_im = [st_ref[:, st0 + i0:st0 + i0 + BLOCK_STATE] for _, i0 in offs]
        sp_ref[half, 0:nb, :] = st_ref[:, st0:st0 + half_cols]
        for pair in range(n_pairs):
            if slot in task_at:
                task_at[slot]()
            slot += 1
            rows = slice((pair + 1) * nb, (pair + 2) * nb)
            for k, (r0, i0) in enumerate(offs):
                n_re = a_re[k] * s_re[k] - a_im[k] * s_im[k] + sp_ref[half, rows, r0:r0 + BLOCK_STATE]
                n_im = a_re[k] * s_im[k] + a_im[k] * s_re[k] + sp_ref[half, rows, i0:i0 + BLOCK_STATE]
                sp_ref[half, rows, r0:r0 + BLOCK_STATE] = n_re
                sp_ref[half, rows, i0:i0 + BLOCK_STATE] = n_im
                s_re[k], s_im[k] = n_re, n_im
        for k, (r0, i0) in enumerate(offs):
            st_ref[:, st0 + r0:st0 + r0 + BLOCK_STATE] = s_re[k]
            st_ref[:, st0 + i0:st0 + i0 + BLOCK_STATE] = s_im[k]
    s_out_ref[...] = st_ref[...]

    y_pair = [
        jnp.dot(sp_ref[sp_block(m, 0)].astype(BF16), ca_ref[m], preferred_element_type=F32)
        + jnp.dot(u_pair[m], t2_ref[m], preferred_element_type=F32)
        for m in range(SSM_BLOCKS)
    ]
    y_even = jnp.concatenate([yp[:, 0:BLOCK_CH] for yp in y_pair], axis=1)
    y_odd = jnp.concatenate([yp[:, BLOCK_CH:2 * BLOCK_CH] for yp in y_pair], axis=1)
    yord_ref[:, 0:nb, :] = y_even.reshape(n_pairs, nb, D_SSM)
    yord_ref[:, nb:2 * nb, :] = y_odd.reshape(n_pairs, nb, D_SSM)
    y_s5 = yord_ref[...].reshape(m_rows, D_SSM)

    sub = m_rows // MIX_SUBTILES
    for s in range(MIX_SUBTILES):
        rs = slice(s * sub, (s + 1) * sub)

        y = y_s5[rs] + dskip_ref[...] * u_ssm[rs]
        y = jax.nn.gelu(y, approximate=True).astype(BF16)
        glu = jnp.dot(y, w_glu_ref[...], preferred_element_type=F32) + b_glu_ref[...]
        y_ssm = glu[:, 0:D_MODEL] * jax.nn.sigmoid(glu[:, D_MODEL:2 * D_MODEL])

        merged = (gate_ref[rs, 0:D_MODEL] * y_ssm + gate_ref[rs, D_MODEL:2 * D_MODEL]).astype(BF16)
        mixed = jnp.dot(merged, w_out_ref[...], preferred_element_type=F32)
        o = h[rs] + _rms(mixed, g_post_ref[...])
        for j in range(N_SLABS):
            o_ref[j, rs, :] = o[:, j * LANES:(j + 1) * LANES]

    tail = ext_ref[m_rows:m_rows + hist_rows, :]
    hist_out_ref[...] = tail
    ext_ref[0:hist_rows, :] = tail


def _mix(h, s0, hist0, p, *, pos0, nb, tc, name):
    n = h.shape[1]
    m_rows = nb * tc
    hist_rows = POOL_MAXW * nb
    kern = functools.partial(_mix_kernel, pos0, nb, tc)
    slabs = pl.BlockSpec((N_SLABS, m_rows, LANES), lambda i: (0, i, 0))
    return pl.pallas_call(
        kern,
        grid=(n // m_rows,),
        in_specs=[
            slabs,
            _const_spec((1, D_MODEL)),
            _const_spec((D_MODEL, 3 * D_MODEL)),
            _const_spec((1, 3 * D_MODEL)),
            _const_spec((nb, STATE_COLS)),
            _const_spec((hist_rows, D_POOL)),
            _const_spec((1, STATE_COLS)),
            _const_spec((SSM_BLOCKS, 2 * BLOCK_CH, 2 * BLOCK_STATE)),
            _const_spec((SSM_BLOCKS, 2 * BLOCK_STATE, 2 * BLOCK_CH)),
            _const_spec((SSM_BLOCKS, 2 * BLOCK_CH, 2 * BLOCK_CH)),
            _const_spec((1, D_SSM)),
            _const_spec((D_SSM, 2 * D_MODEL)),
            _const_spec((1, 2 * D_MODEL)),
            _const_spec((len(POOL_WINDOWS), POOL_GROUP_DIM, POOL_OUT_DIM)),
            _const_spec((1, D_MODEL)),
            _const_spec((D_MODEL, D_MODEL)),
            _const_spec((1, D_MODEL)),
        ],
        out_specs=[
            slabs,
            pl.BlockSpec((nb, STATE_COLS), lambda i: (0, 0)),
            pl.BlockSpec((hist_rows, D_POOL), lambda i: (0, 0)),
        ],
        out_shape=[
            jax.ShapeDtypeStruct((N_SLABS, n, LANES), F32),
            jax.ShapeDtypeStruct((nb, STATE_COLS), F32),
            jax.ShapeDtypeStruct((hist_rows, D_POOL), F32),
        ],
        scratch_shapes=[
            pltpu.VMEM((2, m_rows // 2 + nb, STATE_COLS // 2), F32),
            pltpu.VMEM((m_rows, 2 * D_MODEL), F32),
            pltpu.VMEM((hist_rows + m_rows, D_POOL), F32),
            pltpu.VMEM((nb, STATE_COLS), F32),
            pltpu.VMEM((tc // 2, 2 * nb, D_SSM), F32),
        ],
        compiler_params=pltpu.CompilerParams(
            dimension_semantics=("arbitrary",), vmem_limit_bytes=VMEM_LIMIT_BYTES),
        name=name,
    )(h, p["mix_pre_g"], p["w_in"], p["b_in"], s0, hist0, p["a2"], p["bpair"], p["ca"], p["t2"], p["ssm_d"],
      p["ssm_w_glu"], p["ssm_b_glu"], p["pool_w"], p["pool_scale"], p["w_out"], p["mix_post_g"])


def _block_diag(x):
    nblk, g, r, c = x.shape
    eye = jnp.eye(g, dtype=x.dtype)
    return jnp.einsum("mgrc,gh->mgrhc", x, eye).reshape(nblk, g * r, g * c)


def _ssm_params(lam_re, lam_im, log_step, b_re, b_im, c_re, c_im):
    dt = jnp.exp(log_step)[:, None]
    mag = jnp.exp(lam_re * dt)
    ab_re = mag * jnp.cos(lam_im * dt)
    ab_im = mag * jnp.sin(lam_im * dt)
    den = lam_re * lam_re + lam_im * lam_im
    nr = ab_re - 1.0
    f_re = (nr * lam_re + ab_im * lam_im) / den
    f_im = (ab_im * lam_re - nr * lam_im) / den
    bb_re = f_re[..., None] * b_re - f_im[..., None] * b_im
    bb_im = f_re[..., None] * b_im + f_im[..., None] * b_re

    def blk(x):
        return x.reshape((SSM_BLOCKS, GROUPS_PER_BLOCK) + x.shape[1:])

    def in_map(m_re, m_im):
        return jnp.concatenate([_block_diag(jnp.swapaxes(blk(m_re), 2, 3)),
                                _block_diag(jnp.swapaxes(blk(m_im), 2, 3))], axis=-1)

    def out_map(m_re, m_im):
        return jnp.concatenate([_block_diag(jnp.swapaxes(blk(m_re), 2, 3)),
                                _block_diag(jnp.swapaxes(blk(-m_im), 2, 3))], axis=-2)

    def direct_map(k):
        return _block_diag(jnp.swapaxes(blk(k), 2, 3))

    def cmul(x_re, x_im, y_re, y_im):
        return x_re * y_re - x_im * y_im, x_re * y_im + x_im * y_re

    a2_re, a2_im = cmul(ab_re, ab_im, ab_re, ab_im)
    ab_b_re, ab_b_im = cmul(ab_re[..., None], ab_im[..., None], bb_re, bb_im)
    ca_re, ca_im = cmul(c_re, c_im, ab_re[:, None, :], ab_im[:, None, :])
    ca2_re, ca2_im = cmul(c_re, c_im, a2_re[:, None, :], a2_im[:, None, :])

    def re_prod(m_re, m_im, n_re, n_im):
        hi = lax.Precision.HIGHEST
        return (jnp.einsum("gcp,gpd->gcd", m_re, n_re, precision=hi)
                - jnp.einsum("gcp,gpd->gcd", m_im, n_im, precision=hi))

    k0 = re_prod(c_re, c_im, bb_re, bb_im)
    k1 = re_prod(ca_re, ca_im, bb_re, bb_im)
    bpair = jnp.concatenate([in_map(ab_b_re, ab_b_im), in_map(bb_re, bb_im)], axis=-2).astype(BF16)
    ca = jnp.concatenate([out_map(ca_re, ca_im), out_map(ca2_re, ca2_im)], axis=-1).astype(BF16)
    t2 = jnp.concatenate([
        jnp.concatenate([direct_map(k0), direct_map(k1)], axis=-1),
        jnp.concatenate([jnp.zeros_like(direct_map(k0)), direct_map(k0)], axis=-1),
    ], axis=-2).astype(BF16)
    a2 = jnp.concatenate([blk(a2_re).reshape(SSM_BLOCKS, BLOCK_STATE),
                          blk(a2_im).reshape(SSM_BLOCKS, BLOCK_STATE)], axis=-1).reshape(1, STATE_COLS)
    return a2, bpair, ca, t2


def _state_to_cols(s_re, s_im):
    nb = s_re.shape[0]
    re = s_re.reshape(nb, SSM_BLOCKS, BLOCK_STATE)
    im = s_im.reshape(nb, SSM_BLOCKS, BLOCK_STATE)
    return jnp.stack([re, im], axis=2).reshape(nb, STATE_COLS)


def _cols_to_state(s):
    nb = s.shape[0]
    s = s.reshape(nb, SSM_BLOCKS, 2, GROUPS_PER_BLOCK, SSM_STATE)
    re = s[:, :, 0].reshape(nb, SSM_GROUPS, SSM_STATE)
    im = s[:, :, 1].reshape(nb, SSM_GROUPS, SSM_STATE)
    return re, im


def _layer(x, s0, hist0, p, *, pos0, tt, tc):
    nb, t_len, _ = x.shape
    h = _ffn(x, p["ffn1_pre_g"], p["ffn1_w_in"], p["ffn1_w_out"], p["ffn1_half_post_g"],
             nb=nb, t_len=t_len, tt=tt, rows_in=True, name="ffn1")
    h, s_fin, hist = _mix(h, s0, hist0, p, pos0=pos0, nb=nb, tc=tc, name="mix")
    y = _ffn(h, p["ffn2_pre_g"], p["ffn2_w_in"], p["ffn2_w_out"], p["ffn2_half_post_g"],
             nb=nb, t_len=t_len, tt=tt, rows_in=False, name="ffn2")
    s_re, s_im = _cols_to_state(s_fin)
    pool = jnp.swapaxes(hist.reshape(POOL_MAXW, nb, D_POOL)[1:], 0, 1)
    return y, s_re, s_im, pool


def kernel(x_prompt, x_sample, state_ssm_re, state_ssm_im, cache_pool, ffn1_pre_g, ffn1_w_in, ffn1_w_out, ffn1_post_g, mix_pre_g, w_in, b_in, ssm_lambda_re, ssm_lambda_im, ssm_log_step, ssm_b_re, ssm_b_im, ssm_c_re, ssm_c_im, ssm_d, ssm_w_glu, ssm_b_glu, pool_w, pool_scale, w_out, mix_post_g, ffn2_pre_g, ffn2_w_in, ffn2_w_out, ffn2_post_g):
    depth = ffn1_w_in.shape[0]
    nb_p = x_prompt.shape[0]
    nb_s, t_s, _ = x_sample.shape

    hp, hs = x_prompt, x_sample
    outs = [[] for _ in range(6)]
    for l in range(depth):
        a2, bpair, ca, t2 = _ssm_params(ssm_lambda_re[l], ssm_lambda_im[l], ssm_log_step[l],
                                ssm_b_re[l], ssm_b_im[l], ssm_c_re[l], ssm_c_im[l])
        p = {
            "ffn1_pre_g": ffn1_pre_g[l][None], "ffn1_w_in": ffn1_w_in[l].astype(BF16),
            "ffn1_w_out": ffn1_w_out[l].astype(BF16), "ffn1_half_post_g": 0.5 * ffn1_post_g[l][None],
            "mix_pre_g": mix_pre_g[l][None], "w_in": w_in[l].astype(BF16), "b_in": b_in[l][None],
            "a2": a2, "bpair": bpair, "ca": ca, "t2": t2, "ssm_d": ssm_d[l][None],
            "ssm_w_glu": ssm_w_glu[l].astype(BF16), "ssm_b_glu": ssm_b_glu[l][None],
            "pool_w": pool_w[l].astype(BF16), "pool_scale": pool_scale[l][None],
            "w_out": w_out[l].astype(BF16), "mix_post_g": mix_post_g[l][None],
            "ffn2_pre_g": ffn2_pre_g[l][None], "ffn2_w_in": ffn2_w_in[l].astype(BF16),
            "ffn2_w_out": ffn2_w_out[l].astype(BF16), "ffn2_half_post_g": 0.5 * ffn2_post_g[l][None],
        }
        s0 = jnp.zeros((nb_p, STATE_COLS), F32)
        hist0 = jnp.zeros((POOL_MAXW * nb_p, D_POOL), F32)
        hp, s_re, s_im, pool = _layer(hp, s0, hist0, p, pos0=0, tt=128, tc=64)
        for lst, val in zip(outs[:3], (s_re, s_im, pool)):
            lst.append(val)
        s0 = _state_to_cols(state_ssm_re[l], state_ssm_im[l])
        hist0 = jnp.pad(jnp.swapaxes(cache_pool[l], 0, 1), ((1, 0), (0, 0), (0, 0))).reshape(
            POOL_MAXW * nb_s, D_POOL)
        hs, s_re, s_im, pool = _layer(hs, s0, hist0, p, pos0=PAST_LEN, tt=t_s, tc=t_s)
        for lst, val in zip(outs[3:], (s_re, s_im, pool)):
            lst.append(val)
    return (hp, hs) + tuple(jnp.stack(o) for o in outs)
```

```python
import functools

import jax
import jax.numpy as jnp
from jax import lax
from jax.experimental import pallas as pl
from jax.experimental.pallas import tpu as pltpu

F32 = jnp.float32
BF16 = jnp.bfloat16

LANES = 128
D_MODEL = 1024
D_FF = 2816
D_SSM = 512
D_POOL = 512
SSM_GROUP_DIM = 16
SSM_GROUPS = 32
SSM_STATE = 64
POOL_WINDOWS = (2, 4, 8, 16)
POOL_GROUP_DIM = 128
POOL_OUT_DIM = 256
POOL_MAXW = 16
PAST_LEN = 2048
EPS = 1e-6

N_SLABS = D_MODEL // LANES

SSM_BLOCKS = 4
GROUPS_PER_BLOCK = SSM_GROUPS // SSM_BLOCKS
BLOCK_CH = GROUPS_PER_BLOCK * SSM_GROUP_DIM
BLOCK_STATE = GROUPS_PER_BLOCK * SSM_STATE
STATE_COLS = 2 * SSM_GROUPS * SSM_STATE

FFN_CHUNK = 256
FFN_SUBTILES = 2
MIX_SUBTILES = 2
GATE_CHUNK = 256
VMEM_LIMIT_BYTES = 56 * 1024 * 1024


def _rms(x, g):
    return x * lax.rsqrt(jnp.mean(x * x, axis=-1, keepdims=True) + EPS) * g


def _const_spec(shape):
    zeros = (0,) * len(shape)
    return pl.BlockSpec(shape, lambda i: zeros, pipeline_mode=pl.Buffered(1))


def _ffn_body(x, pre_g_ref, w_in_ref, w_out_ref, half_post_g_ref, xn_ref, g_ref):
    rows = x.shape[0]
    sub = rows // FFN_SUBTILES
    outs = []
    for s in range(FFN_SUBTILES):
        r = slice(s * sub, (s + 1) * sub)
        xs = x[r]
        xn_ref[r, :] = _rms(xs, pre_g_ref[...]).astype(BF16)
        for j in range(D_FF // FFN_CHUNK):
            lo, hi = j * FFN_CHUNK, D_FF + j * FFN_CHUNK
            xn = xn_ref[r, :]
            a = jnp.dot(xn, w_in_ref[:, lo:lo + FFN_CHUNK], preferred_element_type=F32)
            b = jnp.dot(xn, w_in_ref[:, hi:hi + FFN_CHUNK], preferred_element_type=F32)
            g_ref[r, lo:lo + FFN_CHUNK] = (a * jax.nn.sigmoid(a) * b).astype(BF16)
        y = jnp.dot(g_ref[r, :], w_out_ref[...], preferred_element_type=F32)
        outs.append(xs + _rms(y, half_post_g_ref[...]))
    return jnp.concatenate(outs, axis=0)


def _ffn_in_kernel(nb, tt, x_ref, pre_g_ref, w_in_ref, w_out_ref, half_post_g_ref, o_ref, xn_ref, g_ref):
    x = x_ref[...].reshape(nb * tt, D_MODEL)
    o = _ffn_body(x, pre_g_ref, w_in_ref, w_out_ref, half_post_g_ref, xn_ref, g_ref)
    for j in range(N_SLABS):
        for b in range(nb):
            o_ref[j, pl.ds(b, tt, stride=nb), :] = o[b * tt:(b + 1) * tt, j * LANES:(j + 1) * LANES]


def _ffn_out_kernel(nb, tt, x_ref, pre_g_ref, w_in_ref, w_out_ref, half_post_g_ref, o_ref, xn_ref, g_ref):
    x = jnp.concatenate([
        jnp.concatenate([x_ref[j, pl.ds(b, tt, stride=nb), :] for b in range(nb)], axis=0)
        for j in range(N_SLABS)], axis=1)
    o = _ffn_body(x, pre_g_ref, w_in_ref, w_out_ref, half_post_g_ref, xn_ref, g_ref)
    o_ref[...] = o.reshape(nb, tt, D_MODEL)


def _ffn(x, pre_g, w_in, w_out, half_post_g, *, nb, t_len, tt, rows_in, name):
    tm = nb * tt
    natural = pl.BlockSpec((nb, tt, D_MODEL), lambda i: (0, i, 0))
    slabs = pl.BlockSpec((N_SLABS, tm, LANES), lambda i: (0, i, 0))
    natural_shape = jax.ShapeDtypeStruct((nb, t_len, D_MODEL), F32)
    slabs_shape = jax.ShapeDtypeStruct((N_SLABS, t_len * nb, LANES), F32)
    kern = functools.partial(_ffn_in_kernel if rows_in else _ffn_out_kernel, nb, tt)
    return pl.pallas_call(
        kern,
        grid=(t_len // tt,),
        in_specs=[
            natural if rows_in else slabs,
            _const_spec((1, D_MODEL)),
            _const_spec((D_MODEL, 2 * D_FF)),
            _const_spec((D_FF, D_MODEL)),
            _const_spec((1, D_MODEL)),
        ],
        out_specs=slabs if rows_in else natural,
        out_shape=slabs_shape if rows_in else natural_shape,
        scratch_shapes=[
            pltpu.VMEM((tm, D_MODEL), BF16),
            pltpu.VMEM((tm, D_FF), BF16),
        ],
        compiler_params=pltpu.CompilerParams(
            dimension_semantics=("arbitrary",), vmem_limit_bytes=VMEM_LIMIT_BYTES),
        name=name,
    )(x, pre_g, w_in, w_out, half_post_g)


def _mix_kernel(pos0, nb, tc,
                h_ref, g_pre_ref, w_in_ref, b_in_ref, s0_ref, hist0_ref, a2_ref, bpair_ref, ca_ref, t2_ref,
                dskip_ref, w_glu_ref, b_glu_ref, pool_w_ref, pool_scale_ref, w_out_ref, g_post_ref,
                o_ref, s_out_ref, hist_out_ref,
                sp_ref, gate_ref, ext_ref, st_ref, yord_ref):
    m_rows = nb * tc
    n_pairs = tc // 2
    p_rows = nb * n_pairs
    hist_rows = POOL_MAXW * nb
    step = pl.program_id(0)
    half_cols = 2 * 2 * BLOCK_STATE

    @pl.when(step == 0)
    def _():
        st_ref[...] = s0_ref[...]
        ext_ref[0:hist_rows, :] = hist0_ref[...]

    h = jnp.concatenate([h_ref[j] for j in range(N_SLABS)], axis=1)
    z = _rms(h, g_pre_ref[...]).astype(BF16)

    u = jnp.dot(z, w_in_ref[:, 0:D_SSM + D_POOL], preferred_element_type=F32) + b_in_ref[:, 0:D_SSM + D_POOL]
    u_ssm = u[:, 0:D_SSM]
    ext_ref[hist_rows:hist_rows + m_rows, :] = u[:, D_SSM:D_SSM + D_POOL]

    def sp_block(m, first_slab):
        half, k = divmod(m, 2)
        return (half, slice(first_slab * nb, first_slab * nb + p_rows),
                slice(k * 2 * BLOCK_STATE, (k + 1) * 2 * BLOCK_STATE))

    u_pairs = u_ssm.reshape(n_pairs, 2 * nb, D_SSM)
    u_even = u_pairs[:, 0:nb, :].reshape(p_rows, D_SSM).astype(BF16)
    u_odd = u_pairs[:, nb:2 * nb, :].reshape(p_rows, D_SSM).astype(BF16)
    u_pair = [jnp.concatenate([u_even[:, m * BLOCK_CH:(m + 1) * BLOCK_CH],
                               u_odd[:, m * BLOCK_CH:(m + 1) * BLOCK_CH]], axis=1) for m in range(SSM_BLOCKS)]

    for m in range(SSM_BLOCKS):
        sp_ref[sp_block(m, 1)] = jnp.dot(u_pair[m], bpair_ref[m], preferred_element_type=F32)

    def gate(c):
        lo = D_SSM + D_POOL + c * GATE_CHUNK
        gl = jnp.dot(z, w_in_ref[:, lo:lo + GATE_CHUNK], preferred_element_type=F32) + b_in_ref[:, lo:lo + GATE_CHUNK]
        return jax.nn.sigmoid(gl)

    def ssm_gate_task(c):
        def run():
            gate_ref[:, c * GATE_CHUNK:(c + 1) * GATE_CHUNK] = gate(c)
        return run

    def pool_task(gi, w):
        def run():
            sl = slice(gi * POOL_GROUP_DIM, (gi + 1) * POOL_GROUP_DIM)
            row_id = lax.broadcasted_iota(jnp.int32, (m_rows, POOL_GROUP_DIM), 0)
            pos = pos0 + step * tc + row_id // nb
            cur = ext_ref[hist_rows:hist_rows + m_rows, sl]
            tot = cur
            for k in range(1, w):
                tot = tot + ext_ref[hist_rows - k * nb:hist_rows - k * nb + m_rows, sl]
            cnt = jnp.minimum(pos + 1, w).astype(F32)
            pooled = (tot / cnt - cur).astype(BF16)
            osl = slice(gi * POOL_OUT_DIM, (gi + 1) * POOL_OUT_DIM)
            y_pool = jnp.dot(pooled, pool_w_ref[gi], preferred_element_type=F32) * pool_scale_ref[:, osl]
            c = D_MODEL // GATE_CHUNK + gi
            gate_ref[:, c * GATE_CHUNK:(c + 1) * GATE_CHUNK] = gate(c) * y_pool
        return run

    assert POOL_OUT_DIM == GATE_CHUNK
    tasks = []
    for i in range(D_MODEL // GATE_CHUNK):
        tasks += [ssm_gate_task(i), pool_task(i, POOL_WINDOWS[i])]
    n_slots = 2 * n_pairs
    task_at = {(i * n_slots) // len(tasks): t for i, t in enumerate(tasks)}

    slot = 0
    for half in range(2):
        offs = [(k * 2 * BLOCK_STATE, k * 2 * BLOCK_STATE + BLOCK_STATE) for k in range(2)]
        st0 = half * half_cols
        a_re = [jnp.broadcast_to(a2_ref[:, st0 + r0:st0 + r0 + BLOCK_STATE], (nb, BLOCK_STATE)) for r0, _ in offs]
        a_im = [jnp.broadcast_to(a2_ref[:, st0 + i0:st0 + i0 + BLOCK_STATE], (nb, BLOCK_STATE)) for _, i0 in offs]
        s_re = [st_ref[:, st0 + r0:st0 + r0 + BLOCK_STATE] for r0, _ in offs]
        s_im = [st_ref[:, st0 + i0:st0 + i0 + BLOCK_STATE] for _, i0 in offs]
        sp_ref[half, 0:nb, :] = st_ref[:, st0:st0 + half_cols]
        for pair in range(n_pairs):
            if slot in task_at:
                task_at[slot]()
            slot += 1
            rows = slice((pair + 1) * nb, (pair + 2) * nb)
            for k, (r0, i0) in enumerate(offs):
                n_re = a_re[k] * s_re[k] - a_im[k] * s_im[k] + sp_ref[half, rows, r0:r0 + BLOCK_STATE]
                n_im = a_re[k] * s_im[k] + a_im[k] * s_re[k] + sp_ref[half, rows, i0:i0 + BLOCK_STATE]
                sp_ref[half, rows, r0:r0 + BLOCK_STATE] = n_re
                sp_ref[half, rows, i0:i0 + BLOCK_STATE] = n_im
                s_re[k], s_im[k] = n_re, n_im
        for k, (r0, i0) in enumerate(offs):
            st_ref[:, st0 + r0:st0 + r0 + BLOCK_STATE] = s_re[k]
            st_ref[:, st0 + i0:st0 + i0 + BLOCK_STATE] = s_im[k]
    s_out_ref[...] = st_ref[...]

    y_pair = [
        jnp.dot(sp_ref[sp_block(m, 0)].astype(BF16), ca_ref[m], preferred_element_type=F32)
        + jnp.dot(u_pair[m], t2_ref[m], preferred_element_type=F32)
        for m in range(SSM_BLOCKS)
    ]
    y_even = jnp.concatenate([yp[:, 0:BLOCK_CH] for yp in y_pair], axis=1)
    y_odd = jnp.concatenate([yp[:, BLOCK_CH:2 * BLOCK_CH] for yp in y_pair], axis=1)
    yord_ref[:, 0:nb, :] = y_even.reshape(n_pairs, nb, D_SSM)
    yord_ref[:, nb:2 * nb, :] = y_odd.reshape(n_pairs, nb, D_SSM)
    y_s5 = yord_ref[...].reshape(m_rows, D_SSM)

    sub = m_rows // MIX_SUBTILES
    for s in range(MIX_SUBTILES):
        rs = slice(s * sub, (s + 1) * sub)

        y = y_s5[rs] + dskip_ref[...] * u_ssm[rs]
        y = jax.nn.gelu(y, approximate=True).astype(BF16)
        glu = jnp.dot(y, w_glu_ref[...], preferred_element_type=F32) + b_glu_ref[...]
        y_ssm = glu[:, 0:D_MODEL] * jax.nn.sigmoid(glu[:, D_MODEL:2 * D_MODEL])

        merged = (gate_ref[rs, 0:D_MODEL] * y_ssm + gate_ref[rs, D_MODEL:2 * D_MODEL]).astype(BF16)
        mixed = jnp.dot(merged, w_out_ref[...], preferred_element_type=F32)
        o = h[rs] + _rms(mixed, g_post_ref[...])
        for j in range(N_SLABS):
            o_ref[j, rs, :] = o[:, j * LANES:(j + 1) * LANES]

    tail = ext_ref[m_rows:m_rows + hist_rows, :]
    hist_out_ref[...] = tail
    ext_ref[0:hist_rows, :] = tail


def _mix(h, s0, hist0, p, *, pos0, nb, tc, name):
    n = h.shape[1]
    m_rows = nb * tc
    hist_rows = POOL_MAXW * nb
    kern = functools.partial(_mix_kernel, pos0, nb, tc)
    slabs = pl.BlockSpec((N_SLABS, m_rows, LANES), lambda i: (0, i, 0))
    return pl.pallas_call(
        kern,
        grid=(n // m_rows,),
        in_specs=[
            slabs,
            _const_spec((1, D_MODEL)),
            _const_spec((D_MODEL, 3 * D_MODEL)),
            _const_spec((1, 3 * D_MODEL)),
            _const_spec((nb, STATE_COLS)),
            _const_spec((hist_rows, D_POOL)),
            _const_spec((1, STATE_COLS)),
            _const_spec((SSM_BLOCKS, 2 * BLOCK_CH, 2 * BLOCK_STATE)),
            _const_spec((SSM_BLOCKS, 2 * BLOCK_STATE, 2 * BLOCK_CH)),
            _const_spec((SSM_BLOCKS, 2 * BLOCK_CH, 2 * BLOCK_CH)),
            _const_spec((1, D_SSM)),
            _const_spec((D_SSM, 2 * D_MODEL)),
            _const_spec((1, 2 * D_MODEL)),
            _const_spec((len(POOL_WINDOWS), POOL_GROUP_DIM, POOL_OUT_DIM)),
            _const_spec((1, D_MODEL)),
            _const_spec((D_MODEL, D_MODEL)),
            _const_spec((1, D_MODEL)),
        ],
        out_specs=[
            slabs,
            pl.BlockSpec((nb, STATE_COLS), lambda i: (0, 0)),
            pl.BlockSpec((hist_rows, D_POOL), lambda i: (0, 0)),
        ],
        out_shape=[
            jax.ShapeDtypeStruct((N_SLABS, n, LANES), F32),
            jax.ShapeDtypeStruct((nb, STATE_COLS), F32),
            jax.ShapeDtypeStruct((hist_rows, D_POOL), F32),
        ],
        scratch_shapes=[
            pltpu.VMEM((2, m_rows // 2 + nb, STATE_COLS // 2), F32),
            pltpu.VMEM((m_rows, 2 * D_MODEL), F32),
            pltpu.VMEM((hist_rows + m_rows, D_POOL), F32),
            pltpu.VMEM((nb, STATE_COLS), F32),
            pltpu.VMEM((tc // 2, 2 * nb, D_SSM), F32),
        ],
        compiler_params=pltpu.CompilerParams(
            dimension_semantics=("arbitrary",), vmem_limit_bytes=VMEM_LIMIT_BYTES),
        name=name,
    )(h, p["mix_pre_g"], p["w_in"], p["b_in"], s0, hist0, p["a2"], p["bpair"], p["ca"], p["t2"], p["ssm_d"],
      p["ssm_w_glu"], p["ssm_b_glu"], p["pool_w"], p["pool_scale"], p["w_out"], p["mix_post_g"])


def _block_diag(x):
    nblk, g, r, c = x.shape
    eye = jnp.eye(g, dtype=x.dtype)
    return jnp.einsum("mgrc,gh->mgrhc", x, eye).reshape(nblk, g * r, g * c)


def _ssm_params(lam_re, lam_im, log_step, b_re, b_im, c_re, c_im):
    dt = jnp.exp(log_step)[:, None]
    mag = jnp.exp(lam_re * dt)
    ab_re = mag * jnp.cos(lam_im * dt)
    ab_im = mag * jnp.sin(lam_im * dt)
    den = lam_re * lam_re + lam_im * lam_im
    nr = ab_re - 1.0
    f_re = (nr * lam_re + ab_im * lam_im) / den
    f_im = (ab_im * lam_re - nr * lam_im) / den
    bb_re = f_re[..., None] * b_re - f_im[..., None] * b_im
    bb_im = f_re[..., None] * b_im + f_im[..., None] * b_re

    def blk(x):
        return x.reshape((SSM_BLOCKS, GROUPS_PER_BLOCK) + x.shape[1:])

    def in_map(m_re, m_im):
        return jnp.concatenate([_block_diag(jnp.swapaxes(blk(m_re), 2, 3)),
                                _block_diag(jnp.swapaxes(blk(m_im), 2, 3))], axis=-1)

    def out_map(m_re, m_im):
        return jnp.concatenate([_block_diag(jnp.swapaxes(blk(m_re), 2, 3)),
                                _block_diag(jnp.swapaxes(blk(-m_im), 2, 3))], axis=-2)

    def direct_map(k):
        return _block_diag(jnp.swapaxes(blk(k), 2, 3))

    def cmul(x_re, x_im, y_re, y_im):
        return x_re * y_re - x_im * y_im, x_re * y_im + x_im * y_re

    a2_re, a2_im = cmul(ab_re, ab_im, ab_re, ab_im)
    ab_b_re, ab_b_im = cmul(ab_re[..., None], ab_im[..., None], bb_re, bb_im)
    ca_re, ca_im = cmul(c_re, c_im, ab_re[:, None, :], ab_im[:, None, :])
    ca2_re, ca2_im = cmul(c_re, c_im, a2_re[:, None, :], a2_im[:, None, :])

    def re_prod(m_re, m_im, n_re, n_im):
        hi = lax.Precision.HIGHEST
        return (jnp.einsum("gcp,gpd->gcd", m_re, n_re, precision=hi)
                - jnp.einsum("gcp,gpd->gcd", m_im, n_im, precision=hi))

    k0 = re_prod(c_re, c_im, bb_re, bb_im)
    k1 = re_prod(ca_re, ca_im, bb_re, bb_im)
    bpair = jnp.concatenate([in_map(ab_b_re, ab_b_im), in_map(bb_re, bb_im)], axis=-2).astype(BF16)
    ca = jnp.concatenate([out_map(ca_re, ca_im), out_map(ca2_re, ca2_im)], axis=-1).astype(BF16)
    t2 = jnp.concatenate([
        jnp.concatenate([direct_map(k0), direct_map(k1)], axis=-1),
        jnp.concatenate([jnp.zeros_like(direct_map(k0)), direct_map(k0)], axis=-1),
    ], axis=-2).astype(BF16)
    a2 = jnp.concatenate([blk(a2_re).reshape(SSM_BLOCKS, BLOCK_STATE),
                          blk(a2_im).reshape(SSM_BLOCKS, BLOCK_STATE)], axis=-1).reshape(1, STATE_COLS)
    return a2, bpair, ca, t2


def _state_to_cols(s_re, s_im):
    nb = s_re.shape[0]
    re = s_re.reshape(nb, SSM_BLOCKS, BLOCK_STATE)
    im = s_im.reshape(nb, SSM_BLOCKS, BLOCK_STATE)
    return jnp.stack([re, im], axis=2).reshape(nb, STATE_COLS)


def _cols_to_state(s):
    nb = s.shape[0]
    s = s.reshape(nb, SSM_BLOCKS, 2, GROUPS_PER_BLOCK, SSM_STATE)
    re = s[:, :, 0].reshape(nb, SSM_GROUPS, SSM_STATE)
    im = s[:, :, 1].reshape(nb, SSM_GROUPS, SSM_STATE)
    return re, im


def _layer(x, s0, hist0, p, *, pos0, tt, tc):
    nb, t_len, _ = x.shape
    h = _ffn(x, p["ffn1_pre_g"], p["ffn1_w_in"], p["ffn1_w_out"], p["ffn1_half_post_g"],
             nb=nb, t_len=t_len, tt=tt, rows_in=True, name="ffn1")
    h, s_fin, hist = _mix(h, s0, hist0, p, pos0=pos0, nb=nb, tc=tc, name="mix")
    y = _ffn(h, p["ffn2_pre_g"], p["ffn2_w_in"], p["ffn2_w_out"], p["ffn2_half_post_g"],
             nb=nb, t_len=t_len, tt=tt, rows_in=False, name="ffn2")
    s_re, s_im = _cols_to_state(s_fin)
    pool = jnp.swapaxes(hist.reshape(POOL_MAXW, nb, D_POOL)[1:], 0, 1)
    return y, s_re, s_im, pool


def kernel(x_prompt, x_sample, state_ssm_re, state_ssm_im, cache_pool, ffn1_pre_g, ffn1_w_in, ffn1_w_out, ffn1_post_g, mix_pre_g, w_in, b_in, ssm_lambda_re, ssm_lambda_im, ssm_log_step, ssm_b_re, ssm_b_im, ssm_c_re, ssm_c_im, ssm_d, ssm_w_glu, ssm_b_glu, pool_w, pool_scale, w_out, mix_post_g, ffn2_pre_g, ffn2_w_in, ffn2_w_out, ffn2_post_g):
    depth = ffn1_w_in.shape[0]
    nb_p = x_prompt.shape[0]
    nb_s, t_s, _ = x_sample.shape

    hp, hs = x_prompt, x_sample
    outs = [[] for _ in range(6)]
    for l in range(depth):
        a2, bpair, ca, t2 = _ssm_params(ssm_lambda_re[l], ssm_lambda_im[l], ssm_log_step[l],
                                ssm_b_re[l], ssm_b_im[l], ssm_c_re[l], ssm_c_im[l])
        p = {
            "ffn1_pre_g": ffn1_pre_g[l][None], "ffn1_w_in": ffn1_w_in[l].astype(BF16),
            "ffn1_w_out": ffn1_w_out[l].astype(BF16), "ffn1_half_post_g": 0.5 * ffn1_post_g[l][None],
            "mix_pre_g": mix_pre_g[l][None], "w_in": w_in[l].astype(BF16), "b_in": b_in[l][None],
            "a2": a2, "bpair": bpair, "ca": ca, "t2": t2, "ssm_d": ssm_d[l][None],
            "ssm_w_glu": ssm_w_glu[l].astype(BF16), "ssm_b_glu": ssm_b_glu[l][None],
            "pool_w": pool_w[l].astype(BF16), "pool_scale": pool_scale[l][None],
            "w_out": w_out[l].astype(BF16), "mix_post_g": mix_post_g[l][None],
            "ffn2_pre_g": ffn2_pre_g[l][None], "ffn2_w_in": ffn2_w_in[l].astype(BF16),
            "ffn2_w_out": ffn2_w_out[l].astype(BF16), "ffn2_half_post_g": 0.5 * ffn2_post_g[l][None],
        }
        s0 = jnp.zeros((nb_p, STATE_COLS), F32)
        hist0 = jnp.zeros((POOL_MAXW * nb_p, D_POOL), F32)
        hp, s_re, s_im, pool = _layer(hp, s0, hist0, p, pos0=0, tt=128, tc=128)
        for lst, val in zip(outs[:3], (s_re, s_im, pool)):
            lst.append(val)
        s0 = _state_to_cols(state_ssm_re[l], state_ssm_im[l])
        hist0 = jnp.pad(jnp.swapaxes(cache_pool[l], 0, 1), ((1, 0), (0, 0), (0, 0))).reshape(
            POOL_MAXW * nb_s, D_POOL)
        hs, s_re, s_im, pool = _layer(hs, s0, hist0, p, pos0=PAST_LEN, tt=t_s, tc=t_s)
        for lst, val in zip(outs[3:], (s_re, s_im, pool)):
            lst.append(val)
    return (hp, hs) + tuple(jnp.stack(o) for o in outs)
```

```python
import functools

import jax
import jax.numpy as jnp
from jax import lax
from jax.experimental import pallas as pl
from jax.experimental.pallas import tpu as pltpu

F32 = jnp.float32
BF16 = jnp.bfloat16

LANES = 128
D_MODEL = 1024
D_FF = 2816
D_SSM = 512
D_POOL = 512
SSM_GROUP_DIM = 16
SSM_GROUPS = 32
SSM_STATE = 64
POOL_WINDOWS = (2, 4, 8, 16)
POOL_GROUP_DIM = 128
POOL_OUT_DIM = 256
POOL_MAXW = 16
PAST_LEN = 2048
EPS = 1e-6

N_SLABS = D_MODEL // LANES

MXU_DEPTH = 256
S5_UNROLL = 4
BLOCK_CH = MXU_DEPTH // S5_UNROLL
SSM_BLOCKS = D_SSM // BLOCK_CH
GROUPS_PER_BLOCK = SSM_GROUPS // SSM_BLOCKS
BLOCK_STATE = GROUPS_PER_BLOCK * SSM_STATE
STATE_COLS = 2 * SSM_GROUPS * SSM_STATE

FFN_CHUNK = 256
FFN_SUBTILES = 2
MIX_SUBTILES = 2
GATE_CHUNK = 256
VMEM_LIMIT_BYTES = 56 * 1024 * 1024


def _rms(x, g):
    return x * lax.rsqrt(jnp.mean(x * x, axis=-1, keepdims=True) + EPS) * g


def _const_spec(shape):
    zeros = (0,) * len(shape)
    return pl.BlockSpec(shape, lambda i: zeros, pipeline_mode=pl.Buffered(1))


def _ffn_body(x, pre_g_ref, w_in_ref, w_out_ref, half_post_g_ref, xn_ref, g_ref):
    rows = x.shape[0]
    sub = rows // FFN_SUBTILES
    outs = []
    for s in range(FFN_SUBTILES):
        r = slice(s * sub, (s + 1) * sub)
        xs = x[r]
        xn_ref[r, :] = _rms(xs, pre_g_ref[...]).astype(BF16)
        for j in range(D_FF // FFN_CHUNK):
            lo, hi = j * FFN_CHUNK, D_FF + j * FFN_CHUNK
            xn = xn_ref[r, :]
            a = jnp.dot(xn, w_in_ref[:, lo:lo + FFN_CHUNK], preferred_element_type=F32)
            b = jnp.dot(xn, w_in_ref[:, hi:hi + FFN_CHUNK], preferred_element_type=F32)
            g_ref[r, lo:lo + FFN_CHUNK] = (a * jax.nn.sigmoid(a) * b).astype(BF16)
        y = jnp.dot(g_ref[r, :], w_out_ref[...], preferred_element_type=F32)
        outs.append(xs + _rms(y, half_post_g_ref[...]))
    return jnp.concatenate(outs, axis=0)


def _ffn_in_kernel(nb, tt, x_ref, pre_g_ref, w_in_ref, w_out_ref, half_post_g_ref, o_ref, xn_ref, g_ref):
    x = x_ref[...].reshape(nb * tt, D_MODEL)
    o = _ffn_body(x, pre_g_ref, w_in_ref, w_out_ref, half_post_g_ref, xn_ref, g_ref)
    for j in range(N_SLABS):
        for b in range(nb):
            o_ref[j, pl.ds(b, tt, stride=nb), :] = o[b * tt:(b + 1) * tt, j * LANES:(j + 1) * LANES]


def _ffn_out_kernel(nb, tt, x_ref, pre_g_ref, w_in_ref, w_out_ref, half_post_g_ref, o_ref, xn_ref, g_ref):
    x = jnp.concatenate([
        jnp.concatenate([x_ref[j, pl.ds(b, tt, stride=nb), :] for b in range(nb)], axis=0)
        for j in range(N_SLABS)], axis=1)
    o = _ffn_body(x, pre_g_ref, w_in_ref, w_out_ref, half_post_g_ref, xn_ref, g_ref)
    o_ref[...] = o.reshape(nb, tt, D_MODEL)


def _ffn(x, pre_g, w_in, w_out, half_post_g, *, nb, t_len, tt, rows_in, name):
    tm = nb * tt
    natural = pl.BlockSpec((nb, tt, D_MODEL), lambda i: (0, i, 0))
    slabs = pl.BlockSpec((N_SLABS, tm, LANES), lambda i: (0, i, 0))
    natural_shape = jax.ShapeDtypeStruct((nb, t_len, D_MODEL), F32)
    slabs_shape = jax.ShapeDtypeStruct((N_SLABS, t_len * nb, LANES), F32)
    kern = functools.partial(_ffn_in_kernel if rows_in else _ffn_out_kernel, nb, tt)
    return pl.pallas_call(
        kern,
        grid=(t_len // tt,),
        in_specs=[
            natural if rows_in else slabs,
            _const_spec((1, D_MODEL)),
            _const_spec((D_MODEL, 2 * D_FF)),
            _const_spec((D_FF, D_MODEL)),
            _const_spec((1, D_MODEL)),
        ],
        out_specs=slabs if rows_in else natural,
        out_shape=slabs_shape if rows_in else natural_shape,
        scratch_shapes=[
            pltpu.VMEM((tm, D_MODEL), BF16),
            pltpu.VMEM((tm, D_FF), BF16),
        ],
        compiler_params=pltpu.CompilerParams(
            dimension_semantics=("arbitrary",), vmem_limit_bytes=VMEM_LIMIT_BYTES),
        name=name,
    )(x, pre_g, w_in, w_out, half_post_g)


def _mix_kernel(pos0, nb, tc,
                h_ref, g_pre_ref, w_in_ref, b_in_ref, s0_ref, hist0_ref, apow_ref, bmap_ref, cmap_ref, tmap_ref,
                dskip_ref, w_glu_ref, b_glu_ref, pool_w_ref, pool_scale_ref, w_out_ref, g_post_ref,
                o_ref, s_out_ref, hist_out_ref,
                sp_ref, gate_ref, ext_ref, st_ref, yord_ref):
    m_rows = nb * tc
    n_runs = tc // S5_UNROLL
    p_rows = nb * n_runs
    hist_rows = POOL_MAXW * nb
    step = pl.program_id(0)
    blocks_per_half = SSM_BLOCKS // 2
    half_cols = blocks_per_half * 2 * BLOCK_STATE

    @pl.when(step == 0)
    def _():
        st_ref[...] = s0_ref[...]
        ext_ref[0:hist_rows, :] = hist0_ref[...]

    h = jnp.concatenate([h_ref[j] for j in range(N_SLABS)], axis=1)
    z = _rms(h, g_pre_ref[...]).astype(BF16)

    u = jnp.dot(z, w_in_ref[:, 0:D_SSM + D_POOL], preferred_element_type=F32) + b_in_ref[:, 0:D_SSM + D_POOL]
    u_ssm = u[:, 0:D_SSM]
    ext_ref[hist_rows:hist_rows + m_rows, :] = u[:, D_SSM:D_SSM + D_POOL]

    def sp_block(m, first_slab):
        half, k = divmod(m, blocks_per_half)
        return (half, slice(first_slab * nb, first_slab * nb + p_rows),
                slice(k * 2 * BLOCK_STATE, (k + 1) * 2 * BLOCK_STATE))

    u_runs = u_ssm.reshape(n_runs, S5_UNROLL * nb, D_SSM)
    u_at = [u_runs[:, j * nb:(j + 1) * nb, :].reshape(p_rows, D_SSM).astype(BF16) for j in range(S5_UNROLL)]
    u_run = [jnp.concatenate([u_at[j][:, m * BLOCK_CH:(m + 1) * BLOCK_CH] for j in range(S5_UNROLL)], axis=1)
             for m in range(SSM_BLOCKS)]

    for m in range(SSM_BLOCKS):
        sp_ref[sp_block(m, 1)] = jnp.dot(u_run[m], bmap_ref[m], preferred_element_type=F32)

    def gate(c):
        lo = D_SSM + D_POOL + c * GATE_CHUNK
        gl = jnp.dot(z, w_in_ref[:, lo:lo + GATE_CHUNK], preferred_element_type=F32) + b_in_ref[:, lo:lo + GATE_CHUNK]
        return jax.nn.sigmoid(gl)

    def ssm_gate_task(c):
        def run():
            gate_ref[:, c * GATE_CHUNK:(c + 1) * GATE_CHUNK] = gate(c)
        return run

    def pool_task(gi, w):
        def run():
            sl = slice(gi * POOL_GROUP_DIM, (gi + 1) * POOL_GROUP_DIM)
            row_id = lax.broadcasted_iota(jnp.int32, (m_rows, POOL_GROUP_DIM), 0)
            pos = pos0 + step * tc + row_id // nb
            cur = ext_ref[hist_rows:hist_rows + m_rows, sl]
            tot = cur
            for k in range(1, w):
                tot = tot + ext_ref[hist_rows - k * nb:hist_rows - k * nb + m_rows, sl]
            cnt = jnp.minimum(pos + 1, w).astype(F32)
            pooled = (tot / cnt - cur).astype(BF16)
            osl = slice(gi * POOL_OUT_DIM, (gi + 1) * POOL_OUT_DIM)
            y_pool = jnp.dot(pooled, pool_w_ref[gi], preferred_element_type=F32) * pool_scale_ref[:, osl]
            c = D_MODEL // GATE_CHUNK + gi
            gate_ref[:, c * GATE_CHUNK:(c + 1) * GATE_CHUNK] = gate(c) * y_pool
        return run

    assert POOL_OUT_DIM == GATE_CHUNK
    tasks = []
    for i in range(D_MODEL // GATE_CHUNK):
        tasks += [ssm_gate_task(i), pool_task(i, POOL_WINDOWS[i])]
    n_slots = 2 * n_runs
    task_at = {(i * n_slots) // len(tasks): t for i, t in enumerate(tasks)}

    slot = 0
    for half in range(2):
        offs = [(k * 2 * BLOCK_STATE, k * 2 * BLOCK_STATE + BLOCK_STATE) for k in range(blocks_per_half)]
        st0 = half * half_cols
        a_re = [jnp.broadcast_to(apow_ref[:, st0 + r0:st0 + r0 + BLOCK_STATE], (nb, BLOCK_STATE)) for r0, _ in offs]
        a_im = [jnp.broadcast_to(apow_ref[:, st0 + i0:st0 + i0 + BLOCK_STATE], (nb, BLOCK_STATE)) for _, i0 in offs]
        s_re = [st_ref[:, st0 + r0:st0 + r0 + BLOCK_STATE] for r0, _ in offs]
        s_im = [st_ref[:, st0 + i0:st0 + i0 + BLOCK_STATE] for _, i0 in offs]
        sp_ref[half, 0:nb, :] = st_ref[:, st0:st0 + half_cols]
        for run in range(n_runs):
            if slot in task_at:
                task_at[slot]()
            slot += 1
            rows = slice((run + 1) * nb, (run + 2) * nb)
            for k, (r0, i0) in enumerate(offs):
                n_re = a_re[k] * s_re[k] - a_im[k] * s_im[k] + sp_ref[half, rows, r0:r0 + BLOCK_STATE]
                n_im = a_re[k] * s_im[k] + a_im[k] * s_re[k] + sp_ref[half, rows, i0:i0 + BLOCK_STATE]
                sp_ref[half, rows, r0:r0 + BLOCK_STATE] = n_re
                sp_ref[half, rows, i0:i0 + BLOCK_STATE] = n_im
                s_re[k], s_im[k] = n_re, n_im
        for k, (r0, i0) in enumerate(offs):
            st_ref[:, st0 + r0:st0 + r0 + BLOCK_STATE] = s_re[k]
            st_ref[:, st0 + i0:st0 + i0 + BLOCK_STATE] = s_im[k]
    s_out_ref[...] = st_ref[...]

    y_run = [
        jnp.dot(sp_ref[sp_block(m, 0)].astype(BF16), cmap_ref[m], preferred_element_type=F32)
        + jnp.dot(u_run[m], tmap_ref[m], preferred_element_type=F32)
        for m in range(SSM_BLOCKS)
    ]
    for j in range(S5_UNROLL):
        y_at = jnp.concatenate([yr[:, j * BLOCK_CH:(j + 1) * BLOCK_CH] for yr in y_run], axis=1)
        yord_ref[:, j * nb:(j + 1) * nb, :] = y_at.reshape(n_runs, nb, D_SSM)
    y_s5 = yord_ref[...].reshape(m_rows, D_SSM)

    sub = m_rows // MIX_SUBTILES
    for s in range(MIX_SUBTILES):
        rs = slice(s * sub, (s + 1) * sub)

        y = y_s5[rs] + dskip_ref[...] * u_ssm[rs]
        y = jax.nn.gelu(y, approximate=True).astype(BF16)
        glu = jnp.dot(y, w_glu_ref[...], preferred_element_type=F32) + b_glu_ref[...]
        y_ssm = glu[:, 0:D_MODEL] * jax.nn.sigmoid(glu[:, D_MODEL:2 * D_MODEL])

        merged = (gate_ref[rs, 0:D_MODEL] * y_ssm + gate_ref[rs, D_MODEL:2 * D_MODEL]).astype(BF16)
        mixed = jnp.dot(merged, w_out_ref[...], preferred_element_type=F32)
        o = h[rs] + _rms(mixed, g_post_ref[...])
        for j in range(N_SLABS):
            o_ref[j, rs, :] = o[:, j * LANES:(j + 1) * LANES]

    tail = ext_ref[m_rows:m_rows + hist_rows, :]
    hist_out_ref[...] = tail
    ext_ref[0:hist_rows, :] = tail


def _mix(h, s0, hist0, p, *, pos0, nb, tc, name):
    n = h.shape[1]
    m_rows = nb * tc
    hist_rows = POOL_MAXW * nb
    kern = functools.partial(_mix_kernel, pos0, nb, tc)
    slabs = pl.BlockSpec((N_SLABS, m_rows, LANES), lambda i: (0, i, 0))
    return pl.pallas_call(
        kern,
        grid=(n // m_rows,),
        in_specs=[
            slabs,
            _const_spec((1, D_MODEL)),
            _const_spec((D_MODEL, 3 * D_MODEL)),
            _const_spec((1, 3 * D_MODEL)),
            _const_spec((nb, STATE_COLS)),
            _const_spec((hist_rows, D_POOL)),
            _const_spec((1, STATE_COLS)),
            _const_spec((SSM_BLOCKS, MXU_DEPTH, 2 * BLOCK_STATE)),
            _const_spec((SSM_BLOCKS, 2 * BLOCK_STATE, MXU_DEPTH)),
            _const_spec((SSM_BLOCKS, MXU_DEPTH, MXU_DEPTH)),
            _const_spec((1, D_SSM)),
            _const_spec((D_SSM, 2 * D_MODEL)),
            _const_spec((1, 2 * D_MODEL)),
            _const_spec((len(POOL_WINDOWS), POOL_GROUP_DIM, POOL_OUT_DIM)),
            _const_spec((1, D_MODEL)),
            _const_spec((D_MODEL, D_MODEL)),
            _const_spec((1, D_MODEL)),
        ],
        out_specs=[
            slabs,
            pl.BlockSpec((nb, STATE_COLS), lambda i: (0, 0)),
            pl.BlockSpec((hist_rows, D_POOL), lambda i: (0, 0)),
        ],
        out_shape=[
            jax.ShapeDtypeStruct((N_SLABS, n, LANES), F32),
            jax.ShapeDtypeStruct((nb, STATE_COLS), F32),
            jax.ShapeDtypeStruct((hist_rows, D_POOL), F32),
        ],
        scratch_shapes=[
            pltpu.VMEM((2, m_rows // S5_UNROLL + nb, STATE_COLS // 2), F32),
            pltpu.VMEM((m_rows, 2 * D_MODEL), F32),
            pltpu.VMEM((hist_rows + m_rows, D_POOL), F32),
            pltpu.VMEM((nb, STATE_COLS), F32),
            pltpu.VMEM((tc // S5_UNROLL, S5_UNROLL * nb, D_SSM), F32),
        ],
        compiler_params=pltpu.CompilerParams(
            dimension_semantics=("arbitrary",), vmem_limit_bytes=VMEM_LIMIT_BYTES),
        name=name,
    )(h, p["mix_pre_g"], p["w_in"], p["b_in"], s0, hist0, p["apow"], p["bmap"], p["cmap"], p["tmap"], p["ssm_d"],
      p["ssm_w_glu"], p["ssm_b_glu"], p["pool_w"], p["pool_scale"], p["w_out"], p["mix_post_g"])


def _block_diag(x):
    nblk, g, r, c = x.shape
    eye = jnp.eye(g, dtype=x.dtype)
    return jnp.einsum("mgrc,gh->mgrhc", x, eye).reshape(nblk, g * r, g * c)


def _ssm_params(lam_re, lam_im, log_step, b_re, b_im, c_re, c_im):
    dt = jnp.exp(log_step)[:, None]
    mag = jnp.exp(lam_re * dt)
    ab_re = mag * jnp.cos(lam_im * dt)
    ab_im = mag * jnp.sin(lam_im * dt)
    den = lam_re * lam_re + lam_im * lam_im
    nr = ab_re - 1.0
    f_re = (nr * lam_re + ab_im * lam_im) / den
    f_im = (ab_im * lam_re - nr * lam_im) / den
    bb_re = f_re[..., None] * b_re - f_im[..., None] * b_im
    bb_im = f_re[..., None] * b_im + f_im[..., None] * b_re

    def blk(x):
        return x.reshape((SSM_BLOCKS, GROUPS_PER_BLOCK) + x.shape[1:])

    def in_map(m_re, m_im):
        return jnp.concatenate([_block_diag(jnp.swapaxes(blk(m_re), 2, 3)),
                                _block_diag(jnp.swapaxes(blk(m_im), 2, 3))], axis=-1)

    def out_map(m_re, m_im):
        return jnp.concatenate([_block_diag(jnp.swapaxes(blk(m_re), 2, 3)),
                                _block_diag(jnp.swapaxes(blk(-m_im), 2, 3))], axis=-2)

    def direct_map(k):
        return _block_diag(jnp.swapaxes(blk(k), 2, 3))

    def cmul(x_re, x_im, y_re, y_im):
        return x_re * y_re - x_im * y_im, x_re * y_im + x_im * y_re

    pw = [(jnp.ones_like(ab_re), jnp.zeros_like(ab_im))]
    for _ in range(S5_UNROLL):
        pw.append(cmul(pw[-1][0], pw[-1][1], ab_re, ab_im))

    def re_prod(m_re, m_im, n_re, n_im):
        hi = lax.Precision.HIGHEST
        return (jnp.einsum("gcp,gpd->gcd", m_re, n_re, precision=hi)
                - jnp.einsum("gcp,gpd->gcd", m_im, n_im, precision=hi))

    a_b = [cmul(p_re[..., None], p_im[..., None], bb_re, bb_im) for p_re, p_im in pw[:S5_UNROLL]]
    c_a = [cmul(c_re, c_im, p_re[:, None, :], p_im[:, None, :]) for p_re, p_im in pw]
    k = [direct_map(re_prod(c_a[d][0], c_a[d][1], bb_re, bb_im)) for d in range(S5_UNROLL)]
    bmap = jnp.concatenate([in_map(*a_b[S5_UNROLL - 1 - j]) for j in range(S5_UNROLL)], axis=-2).astype(BF16)
    cmap = jnp.concatenate([out_map(*c_a[j + 1]) for j in range(S5_UNROLL)], axis=-1).astype(BF16)
    tmap = jnp.concatenate([
        jnp.concatenate([k[j - i] if j >= i else jnp.zeros_like(k[0]) for j in range(S5_UNROLL)], axis=-1)
        for i in range(S5_UNROLL)], axis=-2).astype(BF16)
    apow = jnp.concatenate([blk(pw[S5_UNROLL][0]).reshape(SSM_BLOCKS, BLOCK_STATE),
                            blk(pw[S5_UNROLL][1]).reshape(SSM_BLOCKS, BLOCK_STATE)], axis=-1).reshape(1, STATE_COLS)
    return apow, bmap, cmap, tmap


def _state_to_cols(s_re, s_im):
    nb = s_re.shape[0]
    re = s_re.reshape(nb, SSM_BLOCKS, BLOCK_STATE)
    im = s_im.reshape(nb, SSM_BLOCKS, BLOCK_STATE)
    return jnp.stack([re, im], axis=2).reshape(nb, STATE_COLS)


def _cols_to_state(s):
    nb = s.shape[0]
    s = s.reshape(nb, SSM_BLOCKS, 2, GROUPS_PER_BLOCK, SSM_STATE)
    re = s[:, :, 0].reshape(nb, SSM_GROUPS, SSM_STATE)
    im = s[:, :, 1].reshape(nb, SSM_GROUPS, SSM_STATE)
    return re, im


def _layer(x, s0, hist0, p, *, pos0, tt, tc):
    nb, t_len, _ = x.shape
    h = _ffn(x, p["ffn1_pre_g"], p["ffn1_w_in"], p["ffn1_w_out"], p["ffn1_half_post_g"],
             nb=nb, t_len=t_len, tt=tt, rows_in=True, name="ffn1")
    h, s_fin, hist = _mix(h, s0, hist0, p, pos0=pos0, nb=nb, tc=tc, name="mix")
    y = _ffn(h, p["ffn2_pre_g"], p["ffn2_w_in"], p["ffn2_w_out"], p["ffn2_half_post_g"],
             nb=nb, t_len=t_len, tt=tt, rows_in=False, name="ffn2")
    s_re, s_im = _cols_to_state(s_fin)
    pool = jnp.swapaxes(hist.reshape(POOL_MAXW, nb, D_POOL)[1:], 0, 1)
    return y, s_re, s_im, pool


def kernel(x_prompt, x_sample, state_ssm_re, state_ssm_im, cache_pool, ffn1_pre_g, ffn1_w_in, ffn1_w_out, ffn1_post_g, mix_pre_g, w_in, b_in, ssm_lambda_re, ssm_lambda_im, ssm_log_step, ssm_b_re, ssm_b_im, ssm_c_re, ssm_c_im, ssm_d, ssm_w_glu, ssm_b_glu, pool_w, pool_scale, w_out, mix_post_g, ffn2_pre_g, ffn2_w_in, ffn2_w_out, ffn2_post_g):
    depth = ffn1_w_in.shape[0]
    nb_p = x_prompt.shape[0]
    nb_s, t_s, _ = x_sample.shape

    hp, hs = x_prompt, x_sample
    outs = [[] for _ in range(6)]
    for l in range(depth):
        apow, bmap, cmap, tmap = _ssm_params(ssm_lambda_re[l], ssm_lambda_im[l], ssm_log_step[l],
                                ssm_b_re[l], ssm_b_im[l], ssm_c_re[l], ssm_c_im[l])
        p = {
            "ffn1_pre_g": ffn1_pre_g[l][None], "ffn1_w_in": ffn1_w_in[l].astype(BF16),
            "ffn1_w_out": ffn1_w_out[l].astype(BF16), "ffn1_half_post_g": 0.5 * ffn1_post_g[l][None],
            "mix_pre_g": mix_pre_g[l][None], "w_in": w_in[l].astype(BF16), "b_in": b_in[l][None],
            "apow": apow, "bmap": bmap, "cmap": cmap, "tmap": tmap, "ssm_d": ssm_d[l][None],
            "ssm_w_glu": ssm_w_glu[l].astype(BF16), "ssm_b_glu": ssm_b_glu[l][None],
            "pool_w": pool_w[l].astype(BF16), "pool_scale": pool_scale[l][None],
            "w_out": w_out[l].astype(BF16), "mix_post_g": mix_post_g[l][None],
            "ffn2_pre_g": ffn2_pre_g[l][None], "ffn2_w_in": ffn2_w_in[l].astype(BF16),
            "ffn2_w_out": ffn2_w_out[l].astype(BF16), "ffn2_half_post_g": 0.5 * ffn2_post_g[l][None],
        }
        s0 = jnp.zeros((nb_p, STATE_COLS), F32)
        hist0 = jnp.zeros((POOL_MAXW * nb_p, D_POOL), F32)
        hp, s_re, s_im, pool = _layer(hp, s0, hist0, p, pos0=0, tt=128, tc=128)
        for lst, val in zip(outs[:3], (s_re, s_im, pool)):
            lst.append(val)
        s0 = _state_to_cols(state_ssm_re[l], state_ssm_im[l])
        hist0 = jnp.pad(jnp.swapaxes(cache_pool[l], 0, 1), ((1, 0), (0, 0), (0, 0))).reshape(
            POOL_MAXW * nb_s, D_POOL)
        hs, s_re, s_im, pool = _layer(hs, s0, hist0, p, pos0=PAST_LEN, tt=t_s, tc=t_s)
        for lst, val in zip(outs[3:], (s_re, s_im, pool)):
            lst.append(val)
    return (hp, hs) + tuple(jnp.stack(o) for o in outs)
```

```python
import functools

import jax
import jax.numpy as jnp
from jax import lax
from jax.experimental import pallas as pl
from jax.experimental.pallas import tpu as pltpu

F32 = jnp.float32
BF16 = jnp.bfloat16

LANES = 128
D_MODEL = 1024
D_FF = 2816
D_SSM = 512
D_POOL = 512
SSM_GROUP_DIM = 16
SSM_GROUPS = 32
SSM_STATE = 64
POOL_WINDOWS = (2, 4, 8, 16)
POOL_GROUP_DIM = 128
POOL_OUT_DIM = 256
POOL_MAXW = 16
PAST_LEN = 2048
EPS = 1e-6

N_SLABS = D_MODEL // LANES

MXU_DEPTH = 256
S5_UNROLL = 8
BLOCK_CH = MXU_DEPTH // S5_UNROLL
SSM_BLOCKS = D_SSM // BLOCK_CH
GROUPS_PER_BLOCK = SSM_GROUPS // SSM_BLOCKS
BLOCK_STATE = GROUPS_PER_BLOCK * SSM_STATE
STATE_COLS = 2 * SSM_GROUPS * SSM_STATE

FFN_CHUNK = 256
FFN_SUBTILES = 2
MIX_SUBTILES = 2
GATE_CHUNK = 256
VMEM_LIMIT_BYTES = 56 * 1024 * 1024


def _rms(x, g):
    return x * lax.rsqrt(jnp.mean(x * x, axis=-1, keepdims=True) + EPS) * g


def _const_spec(shape):
    zeros = (0,) * len(shape)
    return pl.BlockSpec(shape, lambda i: zeros, pipeline_mode=pl.Buffered(1))


def _ffn_body(x, pre_g_ref, w_in_ref, w_out_ref, half_post_g_ref, xn_ref, g_ref):
    rows = x.shape[0]
    sub = rows // FFN_SUBTILES
    outs = []
    for s in range(FFN_SUBTILES):
        r = slice(s * sub, (s + 1) * sub)
        xs = x[r]
        xn_ref[r, :] = _rms(xs, pre_g_ref[...]).astype(BF16)
        for j in range(D_FF // FFN_CHUNK):
            lo, hi = j * FFN_CHUNK, D_FF + j * FFN_CHUNK
            xn = xn_ref[r, :]
            a = jnp.dot(xn, w_in_ref[:, lo:lo + FFN_CHUNK], preferred_element_type=F32)
            b = jnp.dot(xn, w_in_ref[:, hi:hi + FFN_CHUNK], preferred_element_type=F32)
            g_ref[r, lo:lo + FFN_CHUNK] = (a * jax.nn.sigmoid(a) * b).astype(BF16)
        y = jnp.dot(g_ref[r, :], w_out_ref[...], preferred_element_type=F32)
        outs.append(xs + _rms(y, half_post_g_ref[...]))
    return jnp.concatenate(outs, axis=0)


def _ffn_in_kernel(nb, tt, x_ref, pre_g_ref, w_in_ref, w_out_ref, half_post_g_ref, o_ref, xn_ref, g_ref):
    x = x_ref[...].reshape(nb * tt, D_MODEL)
    o = _ffn_body(x, pre_g_ref, w_in_ref, w_out_ref, half_post_g_ref, xn_ref, g_ref)
    for j in range(N_SLABS):
        for b in range(nb):
            o_ref[j, pl.ds(b, tt, stride=nb), :] = o[b * tt:(b + 1) * tt, j * LANES:(j + 1) * LANES]


def _ffn_out_kernel(nb, tt, x_ref, pre_g_ref, w_in_ref, w_out_ref, half_post_g_ref, o_ref, xn_ref, g_ref):
    x = jnp.concatenate([
        jnp.concatenate([x_ref[j, pl.ds(b, tt, stride=nb), :] for b in range(nb)], axis=0)
        for j in range(N_SLABS)], axis=1)
    o = _ffn_body(x, pre_g_ref, w_in_ref, w_out_ref, half_post_g_ref, xn_ref, g_ref)
    o_ref[...] = o.reshape(nb, tt, D_MODEL)


def _ffn(x, pre_g, w_in, w_out, half_post_g, *, nb, t_len, tt, rows_in, name):
    tm = nb * tt
    natural = pl.BlockSpec((nb, tt, D_MODEL), lambda i: (0, i, 0))
    slabs = pl.BlockSpec((N_SLABS, tm, LANES), lambda i: (0, i, 0))
    natural_shape = jax.ShapeDtypeStruct((nb, t_len, D_MODEL), F32)
    slabs_shape = jax.ShapeDtypeStruct((N_SLABS, t_len * nb, LANES), F32)
    kern = functools.partial(_ffn_in_kernel if rows_in else _ffn_out_kernel, nb, tt)
    return pl.pallas_call(
        kern,
        grid=(t_len // tt,),
        in_specs=[
            natural if rows_in else slabs,
            _const_spec((1, D_MODEL)),
            _const_spec((D_MODEL, 2 * D_FF)),
            _const_spec((D_FF, D_MODEL)),
            _const_spec((1, D_MODEL)),
        ],
        out_specs=slabs if rows_in else natural,
        out_shape=slabs_shape if rows_in else natural_shape,
        scratch_shapes=[
            pltpu.VMEM((tm, D_MODEL), BF16),
            pltpu.VMEM((tm, D_FF), BF16),
        ],
        compiler_params=pltpu.CompilerParams(
            dimension_semantics=("arbitrary",), vmem_limit_bytes=VMEM_LIMIT_BYTES),
        name=name,
    )(x, pre_g, w_in, w_out, half_post_g)


def _mix_kernel(pos0, nb, tc,
                h_ref, g_pre_ref, w_in_ref, b_in_ref, s0_ref, hist0_ref, apow_ref, bmap_ref, cmap_ref, tmap_ref,
                dskip_ref, w_glu_ref, b_glu_ref, pool_w_ref, pool_scale_ref, w_out_ref, g_post_ref,
                o_ref, s_out_ref, hist_out_ref,
                sp_ref, gate_ref, ext_ref, st_ref, yord_ref):
    m_rows = nb * tc
    n_runs = tc // S5_UNROLL
    p_rows = nb * n_runs
    hist_rows = POOL_MAXW * nb
    step = pl.program_id(0)
    blocks_per_half = SSM_BLOCKS // 2
    half_cols = blocks_per_half * 2 * BLOCK_STATE

    @pl.when(step == 0)
    def _():
        st_ref[...] = s0_ref[...]
        ext_ref[0:hist_rows, :] = hist0_ref[...]

    h = jnp.concatenate([h_ref[j] for j in range(N_SLABS)], axis=1)
    z = _rms(h, g_pre_ref[...]).astype(BF16)

    u = jnp.dot(z, w_in_ref[:, 0:D_SSM + D_POOL], preferred_element_type=F32) + b_in_ref[:, 0:D_SSM + D_POOL]
    u_ssm = u[:, 0:D_SSM]
    ext_ref[hist_rows:hist_rows + m_rows, :] = u[:, D_SSM:D_SSM + D_POOL]

    def sp_block(m, first_slab):
        half, k = divmod(m, blocks_per_half)
        return (half, slice(first_slab * nb, first_slab * nb + p_rows),
                slice(k * 2 * BLOCK_STATE, (k + 1) * 2 * BLOCK_STATE))

    u_runs = u_ssm.reshape(n_runs, S5_UNROLL * nb, D_SSM)
    u_at = [u_runs[:, j * nb:(j + 1) * nb, :].reshape(p_rows, D_SSM).astype(BF16) for j in range(S5_UNROLL)]
    u_run = [jnp.concatenate([u_at[j][:, m * BLOCK_CH:(m + 1) * BLOCK_CH] for j in range(S5_UNROLL)], axis=1)
             for m in range(SSM_BLOCKS)]

    for m in range(SSM_BLOCKS):
        sp_ref[sp_block(m, 1)] = jnp.dot(u_run[m], bmap_ref[m], preferred_element_type=F32)

    def gate(c):
        lo = D_SSM + D_POOL + c * GATE_CHUNK
        gl = jnp.dot(z, w_in_ref[:, lo:lo + GATE_CHUNK], preferred_element_type=F32) + b_in_ref[:, lo:lo + GATE_CHUNK]
        return jax.nn.sigmoid(gl)

    def ssm_gate_task(c):
        def run():
            gate_ref[:, c * GATE_CHUNK:(c + 1) * GATE_CHUNK] = gate(c)
        return run

    def pool_task(gi, w):
        def run():
            sl = slice(gi * POOL_GROUP_DIM, (gi + 1) * POOL_GROUP_DIM)
            row_id = lax.broadcasted_iota(jnp.int32, (m_rows, POOL_GROUP_DIM), 0)
            pos = pos0 + step * tc + row_id // nb
            cur = ext_ref[hist_rows:hist_rows + m_rows, sl]
            tot = cur
            for k in range(1, w):
                tot = tot + ext_ref[hist_rows - k * nb:hist_rows - k * nb + m_rows, sl]
            cnt = jnp.minimum(pos + 1, w).astype(F32)
            pooled = (tot / cnt - cur).astype(BF16)
            osl = slice(gi * POOL_OUT_DIM, (gi + 1) * POOL_OUT_DIM)
            y_pool = jnp.dot(pooled, pool_w_ref[gi], preferred_element_type=F32) * pool_scale_ref[:, osl]
            c = D_MODEL // GATE_CHUNK + gi
            gate_ref[:, c * GATE_CHUNK:(c + 1) * GATE_CHUNK] = gate(c) * y_pool
        return run

    assert POOL_OUT_DIM == GATE_CHUNK
    tasks = []
    for i in range(D_MODEL // GATE_CHUNK):
        tasks += [ssm_gate_task(i), pool_task(i, POOL_WINDOWS[i])]
    n_slots = 2 * n_runs
    task_at = {(i * n_slots) // len(tasks): t for i, t in enumerate(tasks)}

    slot = 0
    for half in range(2):
        offs = [(k * 2 * BLOCK_STATE, k * 2 * BLOCK_STATE + BLOCK_STATE) for k in range(blocks_per_half)]
        st0 = half * half_cols
        a_re = [jnp.broadcast_to(apow_ref[:, st0 + r0:st0 + r0 + BLOCK_STATE], (nb, BLOCK_STATE)) for r0, _ in offs]
        a_im = [jnp.broadcast_to(apow_ref[:, st0 + i0:st0 + i0 + BLOCK_STATE], (nb, BLOCK_STATE)) for _, i0 in offs]
        s_re = [st_ref[:, st0 + r0:st0 + r0 + BLOCK_STATE] for r0, _ in offs]
        s_im = [st_ref[:, st0 + i0:st0 + i0 + BLOCK_STATE] for _, i0 in offs]
        sp_ref[half, 0:nb, :] = st_ref[:, st0:st0 + half_cols]
        for run in range(n_runs):
            if slot in task_at:
                task_at[slot]()
            slot += 1
            rows = slice((run + 1) * nb, (run + 2) * nb)
            for k, (r0, i0) in enumerate(offs):
                n_re = a_re[k] * s_re[k] - a_im[k] * s_im[k] + sp_ref[half, rows, r0:r0 + BLOCK_STATE]
                n_im = a_re[k] * s_im[k] + a_im[k] * s_re[k] + sp_ref[half, rows, i0:i0 + BLOCK_STATE]
                sp_ref[half, rows, r0:r0 + BLOCK_STATE] = n_re
                sp_ref[half, rows, i0:i0 + BLOCK_STATE] = n_im
                s_re[k], s_im[k] = n_re, n_im
        for k, (r0, i0) in enumerate(offs):
            st_ref[:, st0 + r0:st0 + r0 + BLOCK_STATE] = s_re[k]
            st_ref[:, st0 + i0:st0 + i0 + BLOCK_STATE] = s_im[k]
    s_out_ref[...] = st_ref[...]

    y_run = [
        jnp.dot(sp_ref[sp_block(m, 0)].astype(BF16), cmap_ref[m], preferred_element_type=F32)
        + jnp.dot(u_run[m], tmap_ref[m], preferred_element_type=F32)
        for m in range(SSM_BLOCKS)
    ]
    for j in range(S5_UNROLL):
        y_at = jnp.concatenate([yr[:, j * BLOCK_CH:(j + 1) * BLOCK_CH] for yr in y_run], axis=1)
        yord_ref[:, j * nb:(j + 1) * nb, :] = y_at.reshape(n_runs, nb, D_SSM)
    y_s5 = yord_ref[...].reshape(m_rows, D_SSM)

    sub = m_rows // MIX_SUBTILES
    for s in range(MIX_SUBTILES):
        rs = slice(s * sub, (s + 1) * sub)

        y = y_s5[rs] + dskip_ref[...] * u_ssm[rs]
        y = jax.nn.gelu(y, approximate=True).astype(BF16)
        glu = jnp.dot(y, w_glu_ref[...], preferred_element_type=F32) + b_glu_ref[...]
        y_ssm = glu[:, 0:D_MODEL] * jax.nn.sigmoid(glu[:, D_MODEL:2 * D_MODEL])

        merged = (gate_ref[rs, 0:D_MODEL] * y_ssm + gate_ref[rs, D_MODEL:2 * D_MODEL]).astype(BF16)
        mixed = jnp.dot(merged, w_out_ref[...], preferred_element_type=F32)
        o = h[rs] + _rms(mixed, g_post_ref[...])
        for j in range(N_SLABS):
            o_ref[j, rs, :] = o[:, j * LANES:(j + 1) * LANES]

    tail = ext_ref[m_rows:m_rows + hist_rows, :]
    hist_out_ref[...] = tail
    ext_ref[0:hist_rows, :] = tail


def _mix(h, s0, hist0, p, *, pos0, nb, tc, name):
    n = h.shape[1]
    m_rows = nb * tc
    hist_rows = POOL_MAXW * nb
    kern = functools.partial(_mix_kernel, pos0, nb, tc)
    slabs = pl.BlockSpec((N_SLABS, m_rows, LANES), lambda i: (0, i, 0))
    return pl.pallas_call(
        kern,
        grid=(n // m_rows,),
        in_specs=[
            slabs,
            _const_spec((1, D_MODEL)),
            _const_spec((D_MODEL, 3 * D_MODEL)),
            _const_spec((1, 3 * D_MODEL)),
            _const_spec((nb, STATE_COLS)),
            _const_spec((hist_rows, D_POOL)),
            _const_spec((1, STATE_COLS)),
            _const_spec((SSM_BLOCKS, MXU_DEPTH, 2 * BLOCK_STATE)),
            _const_spec((SSM_BLOCKS, 2 * BLOCK_STATE, MXU_DEPTH)),
            _const_spec((SSM_BLOCKS, MXU_DEPTH, MXU_DEPTH)),
            _const_spec((1, D_SSM)),
            _const_spec((D_SSM, 2 * D_MODEL)),
            _const_spec((1, 2 * D_MODEL)),
            _const_spec((len(POOL_WINDOWS), POOL_GROUP_DIM, POOL_OUT_DIM)),
            _const_spec((1, D_MODEL)),
            _const_spec((D_MODEL, D_MODEL)),
            _const_spec((1, D_MODEL)),
        ],
        out_specs=[
            slabs,
            pl.BlockSpec((nb, STATE_COLS), lambda i: (0, 0)),
            pl.BlockSpec((hist_rows, D_POOL), lambda i: (0, 0)),
        ],
        out_shape=[
            jax.ShapeDtypeStruct((N_SLABS, n, LANES), F32),
            jax.ShapeDtypeStruct((nb, STATE_COLS), F32),
            jax.ShapeDtypeStruct((hist_rows, D_POOL), F32),
        ],
        scratch_shapes=[
            pltpu.VMEM((2, m_rows // S5_UNROLL + nb, STATE_COLS // 2), F32),
            pltpu.VMEM((m_rows, 2 * D_MODEL), F32),
            pltpu.VMEM((hist_rows + m_rows, D_POOL), F32),
            pltpu.VMEM((nb, STATE_COLS), F32),
            pltpu.VMEM((tc // S5_UNROLL, S5_UNROLL * nb, D_SSM), F32),
        ],
        compiler_params=pltpu.CompilerParams(
            dimension_semantics=("arbitrary",), vmem_limit_bytes=VMEM_LIMIT_BYTES),
        name=name,
    )(h, p["mix_pre_g"], p["w_in"], p["b_in"], s0, hist0, p["apow"], p["bmap"], p["cmap"], p["tmap"], p["ssm_d"],
      p["ssm_w_glu"], p["ssm_b_glu"], p["pool_w"], p["pool_scale"], p["w_out"], p["mix_post_g"])


def _block_diag(x):
    nblk, g, r, c = x.shape
    eye = jnp.eye(g, dtype=x.dtype)
    return jnp.einsum("mgrc,gh->mgrhc", x, eye).reshape(nblk, g * r, g * c)


def _ssm_params(lam_re, lam_im, log_step, b_re, b_im, c_re, c_im):
    dt = jnp.exp(log_step)[:, None]
    mag = jnp.exp(lam_re * dt)
    ab_re = mag * jnp.cos(lam_im * dt)
    ab_im = mag * jnp.sin(lam_im * dt)
    den = lam_re * lam_re + lam_im * lam_im
    nr = ab_re - 1.0
    f_re = (nr * lam_re + ab_im * lam_im) / den
    f_im = (ab_im * lam_re - nr * lam_im) / den
    bb_re = f_re[..., None] * b_re - f_im[..., None] * b_im
    bb_im = f_re[..., None] * b_im + f_im[..., None] * b_re

    def blk(x):
        return x.reshape((SSM_BLOCKS, GROUPS_PER_BLOCK) + x.shape[1:])

    def in_map(m_re, m_im):
        return jnp.concatenate([_block_diag(jnp.swapaxes(blk(m_re), 2, 3)),
                                _block_diag(jnp.swapaxes(blk(m_im), 2, 3))], axis=-1)

    def out_map(m_re, m_im):
        return jnp.concatenate([_block_diag(jnp.swapaxes(blk(m_re), 2, 3)),
                                _block_diag(jnp.swapaxes(blk(-m_im), 2, 3))], axis=-2)

    def direct_map(k):
        return _block_diag(jnp.swapaxes(blk(k), 2, 3))

    def cmul(x_re, x_im, y_re, y_im):
        return x_re * y_re - x_im * y_im, x_re * y_im + x_im * y_re

    pw = [(jnp.ones_like(ab_re), jnp.zeros_like(ab_im))]
    for _ in range(S5_UNROLL):
        pw.append(cmul(pw[-1][0], pw[-1][1], ab_re, ab_im))

    def re_prod(m_re, m_im, n_re, n_im):
        hi = lax.Precision.HIGHEST
        return (jnp.einsum("gcp,gpd->gcd", m_re, n_re, precision=hi)
                - jnp.einsum("gcp,gpd->gcd", m_im, n_im, precision=hi))

    a_b = [cmul(p_re[..., None], p_im[..., None], bb_re, bb_im) for p_re, p_im in pw[:S5_UNROLL]]
    c_a = [cmul(c_re, c_im, p_re[:, None, :], p_im[:, None, :]) for p_re, p_im in pw]
    k = [direct_map(re_prod(c_a[d][0], c_a[d][1], bb_re, bb_im)) for d in range(S5_UNROLL)]
    bmap = jnp.concatenate([in_map(*a_b[S5_UNROLL - 1 - j]) for j in range(S5_UNROLL)], axis=-2).astype(BF16)
    cmap = jnp.concatenate([out_map(*c_a[j + 1]) for j in range(S5_UNROLL)], axis=-1).astype(BF16)
    tmap = jnp.concatenate([
        jnp.concatenate([k[j - i] if j >= i else jnp.zeros_like(k[0]) for j in range(S5_UNROLL)], axis=-1)
        for i in range(S5_UNROLL)], axis=-2).astype(BF16)
    apow = jnp.concatenate([blk(pw[S5_UNROLL][0]).reshape(SSM_BLOCKS, BLOCK_STATE),
                            blk(pw[S5_UNROLL][1]).reshape(SSM_BLOCKS, BLOCK_STATE)], axis=-1).reshape(1, STATE_COLS)
    return apow, bmap, cmap, tmap


def _state_to_cols(s_re, s_im):
    nb = s_re.shape[0]
    re = s_re.reshape(nb, SSM_BLOCKS, BLOCK_STATE)
    im = s_im.reshape(nb, SSM_BLOCKS, BLOCK_STATE)
    return jnp.stack([re, im], axis=2).reshape(nb, STATE_COLS)


def _cols_to_state(s):
    nb = s.shape[0]
    s = s.reshape(nb, SSM_BLOCKS, 2, GROUPS_PER_BLOCK, SSM_STATE)
    re = s[:, :, 0].reshape(nb, SSM_GROUPS, SSM_STATE)
    im = s[:, :, 1].reshape(nb, SSM_GROUPS, SSM_STATE)
    return re, im


def _layer(x, s0, hist0, p, *, pos0, tt, tc):
    nb, t_len, _ = x.shape
    h = _ffn(x, p["ffn1_pre_g"], p["ffn1_w_in"], p["ffn1_w_out"], p["ffn1_half_post_g"],
             nb=nb, t_len=t_len, tt=tt, rows_in=True, name="ffn1")
    h, s_fin, hist = _mix(h, s0, hist0, p, pos0=pos0, nb=nb, tc=tc, name="mix")
    y = _ffn(h, p["ffn2_pre_g"], p["ffn2_w_in"], p["ffn2_w_out"], p["ffn2_half_post_g"],
             nb=nb, t_len=t_len, tt=tt, rows_in=False, name="ffn2")
    s_re, s_im = _cols_to_state(s_fin)
    pool = jnp.swapaxes(hist.reshape(POOL_MAXW, nb, D_POOL)[1:], 0, 1)
    return y, s_re, s_im, pool


def kernel(x_prompt, x_sample, state_ssm_re, state_ssm_im, cache_pool, ffn1_pre_g, ffn1_w_in, ffn1_w_out, ffn1_post_g, mix_pre_g, w_in, b_in, ssm_lambda_re, ssm_lambda_im, ssm_log_step, ssm_b_re, ssm_b_im, ssm_c_re, ssm_c_im, ssm_d, ssm_w_glu, ssm_b_glu, pool_w, pool_scale, w_out, mix_post_g, ffn2_pre_g, ffn2_w_in, ffn2_w_out, ffn2_post_g):
    depth = ffn1_w_in.shape[0]
    nb_p = x_prompt.shape[0]
    nb_s, t_s, _ = x_sample.shape

    hp, hs = x_prompt, x_sample
    outs = [[] for _ in range(6)]
    for l in range(depth):
        apow, bmap, cmap, tmap = _ssm_params(ssm_lambda_re[l], ssm_lambda_im[l], ssm_log_step[l],
                                ssm_b_re[l], ssm_b_im[l], ssm_c_re[l], ssm_c_im[l])
        p = {
            "ffn1_pre_g": ffn1_pre_g[l][None], "ffn1_w_in": ffn1_w_in[l].astype(BF16),
            "ffn1_w_out": ffn1_w_out[l].astype(BF16), "ffn1_half_post_g": 0.5 * ffn1_post_g[l][None],
            "mix_pre_g": mix_pre_g[l][None], "w_in": w_in[l].astype(BF16), "b_in": b_in[l][None],
            "apow": apow, "bmap": bmap, "cmap": cmap, "tmap": tmap, "ssm_d": ssm_d[l][None],
            "ssm_w_glu": ssm_w_glu[l].astype(BF16), "ssm_b_glu": ssm_b_glu[l][None],
            "pool_w": pool_w[l].astype(BF16), "pool_scale": pool_scale[l][None],
            "w_out": w_out[l].astype(BF16), "mix_post_g": mix_post_g[l][None],
            "ffn2_pre_g": ffn2_pre_g[l][None], "ffn2_w_in": ffn2_w_in[l].astype(BF16),
            "ffn2_w_out": ffn2_w_out[l].astype(BF16), "ffn2_half_post_g": 0.5 * ffn2_post_g[l][None],
        }
        s0 = jnp.zeros((nb_p, STATE_COLS), F32)
        hist0 = jnp.zeros((POOL_MAXW * nb_p, D_POOL), F32)
        hp, s_re, s_im, pool = _layer(hp, s0, hist0, p, pos0=0, tt=128, tc=128)
        for lst, val in zip(outs[:3], (s_re, s_im, pool)):
            lst.append(val)
        s0 = _state_to_cols(state_ssm_re[l], state_ssm_im[l])
        hist0 = jnp.pad(jnp.swapaxes(cache_pool[l], 0, 1), ((1, 0), (0, 0), (0, 0))).reshape(
            POOL_MAXW * nb_s, D_POOL)
        hs, s_re, s_im, pool = _layer(hs, s0, hist0, p, pos0=PAST_LEN, tt=t_s, tc=t_s)
        for lst, val in zip(outs[3:], (s_re, s_im, pool)):
            lst.append(val)
    return (hp, hs) + tuple(jnp.stack(o) for o in outs)
```

```python
import functools

import jax
import jax.numpy as jnp
from jax import lax
from jax.experimental import pallas as pl
from jax.experimental.pallas import tpu as pltpu

F32 = jnp.float32
BF16 = jnp.bfloat16

LANES = 128
D_MODEL = 1024
D_FF = 2816
D_SSM = 512
D_POOL = 512
SSM_GROUP_DIM = 16
SSM_GROUPS = 32
SSM_STATE = 64
POOL_WINDOWS = (2, 4, 8, 16)
POOL_GROUP_DIM = 128
POOL_OUT_DIM = 256
POOL_MAXW = 16
PAST_LEN = 2048
EPS = 1e-6

N_SLABS = D_MODEL // LANES

MXU_DEPTH = 256
S5_UNROLL = 8
BLOCK_CH = MXU_DEPTH // S5_UNROLL
SSM_BLOCKS = D_SSM // BLOCK_CH
GROUPS_PER_BLOCK = SSM_GROUPS // SSM_BLOCKS
BLOCK_STATE = GROUPS_PER_BLOCK * SSM_STATE
STATE_COLS = 2 * SSM_GROUPS * SSM_STATE

FFN_CHUNK = 256
FFN_SUBTILES = 2
MIX_SUBTILES = 2
GATE_CHUNK = 256
VMEM_LIMIT_BYTES = 56 * 1024 * 1024


def _rms(x, g):
    return x * lax.rsqrt(jnp.mean(x * x, axis=-1, keepdims=True) + EPS) * g


def _const_spec(shape):
    zeros = (0,) * len(shape)
    return pl.BlockSpec(shape, lambda i: zeros, pipeline_mode=pl.Buffered(1))


def _ffn_body(x, pre_g_ref, w_in_ref, w_out_ref, half_post_g_ref, xn_ref, g_ref):
    rows = x.shape[0]
    sub = rows // FFN_SUBTILES
    outs = []
    for s in range(FFN_SUBTILES):
        r = slice(s * sub, (s + 1) * sub)
        xs = x[r]
        xn_ref[r, :] = _rms(xs, pre_g_ref[...]).astype(BF16)
        for j in range(D_FF // FFN_CHUNK):
            lo, hi = j * FFN_CHUNK, D_FF + j * FFN_CHUNK
            xn = xn_ref[r, :]
            a = jnp.dot(xn, w_in_ref[:, lo:lo + FFN_CHUNK], preferred_element_type=F32)
            b = jnp.dot(xn, w_in_ref[:, hi:hi + FFN_CHUNK], preferred_element_type=F32)
            g_ref[r, lo:lo + FFN_CHUNK] = (a * jax.nn.sigmoid(a) * b).astype(BF16)
        y = jnp.dot(g_ref[r, :], w_out_ref[...], preferred_element_type=F32)
        outs.append(xs + _rms(y, half_post_g_ref[...]))
    return jnp.concatenate(outs, axis=0)


def _ffn_in_kernel(nb, tt, x_ref, pre_g_ref, w_in_ref, w_out_ref, half_post_g_ref, o_ref, xn_ref, g_ref):
    x = x_ref[...].reshape(nb * tt, D_MODEL)
    o = _ffn_body(x, pre_g_ref, w_in_ref, w_out_ref, half_post_g_ref, xn_ref, g_ref)
    for j in range(N_SLABS):
        for b in range(nb):
            o_ref[j, pl.ds(b, tt, stride=nb), :] = o[b * tt:(b + 1) * tt, j * LANES:(j + 1) * LANES]


def _ffn_out_kernel(nb, tt, x_ref, pre_g_ref, w_in_ref, w_out_ref, half_post_g_ref, o_ref, xn_ref, g_ref):
    x = jnp.concatenate([
        jnp.concatenate([x_ref[j, pl.ds(b, tt, stride=nb), :] for b in range(nb)], axis=0)
        for j in range(N_SLABS)], axis=1)
    o = _ffn_body(x, pre_g_ref, w_in_ref, w_out_ref, half_post_g_ref, xn_ref, g_ref)
    o_ref[...] = o.reshape(nb, tt, D_MODEL)


def _ffn(x, pre_g, w_in, w_out, half_post_g, *, nb, t_len, tt, rows_in, name):
    tm = nb * tt
    natural = pl.BlockSpec((nb, tt, D_MODEL), lambda i: (0, i, 0))
    slabs = pl.BlockSpec((N_SLABS, tm, LANES), lambda i: (0, i, 0))
    natural_shape = jax.ShapeDtypeStruct((nb, t_len, D_MODEL), F32)
    slabs_shape = jax.ShapeDtypeStruct((N_SLABS, t_len * nb, LANES), F32)
    kern = functools.partial(_ffn_in_kernel if rows_in else _ffn_out_kernel, nb, tt)
    return pl.pallas_call(
        kern,
        grid=(t_len // tt,),
        in_specs=[
            natural if rows_in else slabs,
            _const_spec((1, D_MODEL)),
            _const_spec((D_MODEL, 2 * D_FF)),
            _const_spec((D_FF, D_MODEL)),
            _const_spec((1, D_MODEL)),
        ],
        out_specs=slabs if rows_in else natural,
        out_shape=slabs_shape if rows_in else natural_shape,
        scratch_shapes=[
            pltpu.VMEM((tm, D_MODEL), BF16),
            pltpu.VMEM((tm, D_FF), BF16),
        ],
        compiler_params=pltpu.CompilerParams(
            dimension_semantics=("arbitrary",), vmem_limit_bytes=VMEM_LIMIT_BYTES),
        name=name,
    )(x, pre_g, w_in, w_out, half_post_g)


def _mix_kernel(pos0, nb, tc,
                h_ref, g_pre_ref, w_in_ref, b_in_ref, s0_ref, hist0_ref, apow_ref, bmap_ref, cmap_ref, tmap_ref,
                dskip_ref, w_glu_ref, b_glu_ref, pool_w_ref, pool_scale_ref, w_out_ref, g_post_ref,
                o_ref, s_out_ref, hist_out_ref,
                sp_ref, gate_ref, ext_ref, st_ref, yord_ref):
    m_rows = nb * tc
    n_runs = tc // S5_UNROLL
    p_rows = nb * n_runs
    hist_rows = POOL_MAXW * nb
    step = pl.program_id(0)
    blocks_per_half = SSM_BLOCKS // 2
    half_cols = blocks_per_half * 2 * BLOCK_STATE

    @pl.when(step == 0)
    def _():
        st_ref[...] = s0_ref[...]
        ext_ref[0:hist_rows, :] = hist0_ref[...]

    h = jnp.concatenate([h_ref[j] for j in range(N_SLABS)], axis=1)
    z = _rms(h, g_pre_ref[...]).astype(BF16)

    u = jnp.dot(z, w_in_ref[:, 0:D_SSM + D_POOL], preferred_element_type=F32) + b_in_ref[:, 0:D_SSM + D_POOL]
    u_ssm = u[:, 0:D_SSM]
    ext_ref[hist_rows:hist_rows + m_rows, :] = u[:, D_SSM:D_SSM + D_POOL]

    def sp_block(m, first_slab):
        half, k = divmod(m, blocks_per_half)
        return (half, slice(first_slab * nb, first_slab * nb + p_rows),
                slice(k * 2 * BLOCK_STATE, (k + 1) * 2 * BLOCK_STATE))

    u_runs = u_ssm.reshape(n_runs, S5_UNROLL * nb, D_SSM)
    u_at = [u_runs[:, j * nb:(j + 1) * nb, :].reshape(p_rows, D_SSM).astype(BF16) for j in range(S5_UNROLL)]
    u_run = [jnp.concatenate([u_at[j][:, m * BLOCK_CH:(m + 1) * BLOCK_CH] for j in range(S5_UNROLL)], axis=1)
             for m in range(SSM_BLOCKS)]

    for m in range(SSM_BLOCKS):
        sp_ref[sp_block(m, 1)] = jnp.dot(u_run[m], bmap_ref[m], preferred_element_type=F32)

    def gate(c):
        lo = D_SSM + D_POOL + c * GATE_CHUNK
        gl = jnp.dot(z, w_in_ref[:, lo:lo + GATE_CHUNK], preferred_element_type=F32) + b_in_ref[:, lo:lo + GATE_CHUNK]
        return jax.nn.sigmoid(gl)

    def ssm_gate_task(c):
        def run():
            gate_ref[:, c * GATE_CHUNK:(c + 1) * GATE_CHUNK] = gate(c)
        return run

    def pool_task(gi, w):
        def run():
            sl = slice(gi * POOL_GROUP_DIM, (gi + 1) * POOL_GROUP_DIM)
            row_id = lax.broadcasted_iota(jnp.int32, (m_rows, POOL_GROUP_DIM), 0)
            pos = pos0 + step * tc + row_id // nb
            cur = ext_ref[hist_rows:hist_rows + m_rows, sl]
            tot = cur
            for k in range(1, w):
                tot = tot + ext_ref[hist_rows - k * nb:hist_rows - k * nb + m_rows, sl]
            cnt = jnp.minimum(pos + 1, w).astype(F32)
            pooled = (tot / cnt - cur).astype(BF16)
            osl = slice(gi * POOL_OUT_DIM, (gi + 1) * POOL_OUT_DIM)
            y_pool = jnp.dot(pooled, pool_w_ref[gi], preferred_element_type=F32) * pool_scale_ref[:, osl]
            c = D_MODEL // GATE_CHUNK + gi
            gate_ref[:, c * GATE_CHUNK:(c + 1) * GATE_CHUNK] = gate(c) * y_pool
        return run

    assert POOL_OUT_DIM == GATE_CHUNK
    tasks = []
    for i in range(D_MODEL // GATE_CHUNK):
        tasks += [ssm_gate_task(i), pool_task(i, POOL_WINDOWS[i])]
    n_slots = 2 * n_runs
    task_at = {(i * n_slots) // len(tasks): t for i, t in enumerate(tasks)}

    slot = 0
    for half in range(2):
        offs = [(k * 2 * BLOCK_STATE, k * 2 * BLOCK_STATE + BLOCK_STATE) for k in range(blocks_per_half)]
        st0 = half * half_cols
        a_re = [jnp.broadcast_to(apow_ref[:, st0 + r0:st0 + r0 + BLOCK_STATE], (nb, BLOCK_STATE)) for r0, _ in offs]
        a_im = [jnp.broadcast_to(apow_ref[:, st0 + i0:st0 + i0 + BLOCK_STATE], (nb, BLOCK_STATE)) for _, i0 in offs]
        s_re = [st_ref[:, st0 + r0:st0 + r0 + BLOCK_STATE] for r0, _ in offs]
        s_im = [st_ref[:, st0 + i0:st0 + i0 + BLOCK_STATE] for _, i0 in offs]
        sp_ref[half, 0:nb, :] = st_ref[:, st0:st0 + half_cols]
        for run in range(n_runs):
            if slot in task_at:
                task_at[slot]()
            slot += 1
            rows = slice((run + 1) * nb, (run + 2) * nb)
            for k, (r0, i0) in enumerate(offs):
                n_re = a_re[k] * s_re[k] - a_im[k] * s_im[k] + sp_ref[half, rows, r0:r0 + BLOCK_STATE]
                n_im = a_re[k] * s_im[k] + a_im[k] * s_re[k] + sp_ref[half, rows, i0:i0 + BLOCK_STATE]
                sp_ref[half, rows, r0:r0 + BLOCK_STATE] = n_re
                sp_ref[half, rows, i0:i0 + BLOCK_STATE] = n_im
                s_re[k], s_im[k] = n_re, n_im
        for k, (r0, i0) in enumerate(offs):
            st_ref[:, st0 + r0:st0 + r0 + BLOCK_STATE] = s_re[k]
            st_ref[:, st0 + i0:st0 + i0 + BLOCK_STATE] = s_im[k]
    s_out_ref[...] = st_ref[...]

    y_run = [
        jnp.dot(sp_ref[sp_block(m, 0)].astype(BF16), cmap_ref[m], preferred_element_type=F32)
        + jnp.dot(u_run[m], tmap_ref[m], preferred_element_type=F32)
        for m in range(SSM_BLOCKS)
    ]
    for j in range(S5_UNROLL):
        y_at = jnp.concatenate([yr[:, j * BLOCK_CH:(j + 1) * BLOCK_CH] for yr in y_run], axis=1)
        yord_ref[:, j * nb:(j + 1) * nb, :] = y_at.reshape(n_runs, nb, D_SSM)
    y_s5 = yord_ref[...].reshape(m_rows, D_SSM)

    sub = m_rows // MIX_SUBTILES
    for s in range(MIX_SUBTILES):
        rs = slice(s * sub, (s + 1) * sub)

        y = y_s5[rs] + dskip_ref[...] * u_ssm[rs]
        y = jax.nn.gelu(y, approximate=True).astype(BF16)
        glu = jnp.dot(y, w_glu_ref[...], preferred_element_type=F32) + b_glu_ref[...]
        y_ssm = glu[:, 0:D_MODEL] * jax.nn.sigmoid(glu[:, D_MODEL:2 * D_MODEL])

        merged = (gate_ref[rs, 0:D_MODEL] * y_ssm + gate_ref[rs, D_MODEL:2 * D_MODEL]).astype(BF16)
        mixed = jnp.dot(merged, w_out_ref[...], preferred_element_type=F32)
        o = h[rs] + _rms(mixed, g_post_ref[...])
        for j in range(N_SLABS):
            o_ref[j, rs, :] = o[:, j * LANES:(j + 1) * LANES]

    tail = ext_ref[m_rows:m_rows + hist_rows, :]
    hist_out_ref[...] = tail
    ext_ref[0:hist_rows, :] = tail


def _mix(h, s0, hist0, p, *, pos0, nb, tc, name):
    n = h.shape[1]
    m_rows = nb * tc
    hist_rows = POOL_MAXW * nb
    kern = functools.partial(_mix_kernel, pos0, nb, tc)
    slabs = pl.BlockSpec((N_SLABS, m_rows, LANES), lambda i: (0, i, 0))
    return pl.pallas_call(
        kern,
        grid=(n // m_rows,),
        in_specs=[
            slabs,
            _const_spec((1, D_MODEL)),
            _const_spec((D_MODEL, 3 * D_MODEL)),
            _const_spec((1, 3 * D_MODEL)),
            _const_spec((nb, STATE_COLS)),
            _const_spec((hist_rows, D_POOL)),
            _const_spec((1, STATE_COLS)),
            _const_spec((SSM_BLOCKS, MXU_DEPTH, 2 * BLOCK_STATE)),
            _const_spec((SSM_BLOCKS, 2 * BLOCK_STATE, MXU_DEPTH)),
            _const_spec((SSM_BLOCKS, MXU_DEPTH, MXU_DEPTH)),
            _const_spec((1, D_SSM)),
            _const_spec((D_SSM, 2 * D_MODEL)),
            _const_spec((1, 2 * D_MODEL)),
            _const_spec((len(POOL_WINDOWS), POOL_GROUP_DIM, POOL_OUT_DIM)),
            _const_spec((1, D_MODEL)),
            _const_spec((D_MODEL, D_MODEL)),
            _const_spec((1, D_MODEL)),
        ],
        out_specs=[
            slabs,
            pl.BlockSpec((nb, STATE_COLS), lambda i: (0, 0)),
            pl.BlockSpec((hist_rows, D_POOL), lambda i: (0, 0)),
        ],
        out_shape=[
            jax.ShapeDtypeStruct((N_SLABS, n, LANES), F32),
            jax.ShapeDtypeStruct((nb, STATE_COLS), F32),
            jax.ShapeDtypeStruct((hist_rows, D_POOL), F32),
        ],
        scratch_shapes=[
            pltpu.VMEM((2, m_rows // S5_UNROLL + nb, STATE_COLS // 2), F32),
            pltpu.VMEM((m_rows, 2 * D_MODEL), F32),
            pltpu.VMEM((hist_rows + m_rows, D_POOL), F32),
            pltpu.VMEM((nb, STATE_COLS), F32),
            pltpu.VMEM((tc // S5_UNROLL, S5_UNROLL * nb, D_SSM), F32),
        ],
        compiler_params=pltpu.CompilerParams(
            dimension_semantics=("arbitrary",), vmem_limit_bytes=VMEM_LIMIT_BYTES),
        name=name,
    )(h, p["mix_pre_g"], p["w_in"], p["b_in"], s0, hist0, p["apow"], p["bmap"], p["cmap"], p["tmap"], p["ssm_d"],
      p["ssm_w_glu"], p["ssm_b_glu"], p["pool_w"], p["pool_scale"], p["w_out"], p["mix_post_g"])


def _block_diag(x):
    nblk, g, r, c = x.shape
    eye = jnp.eye(g, dtype=x.dtype)
    return (x[:, :, :, None, :] * eye[None, :, None, :, None]).reshape(nblk, g * r, g * c)


def _ssm_params(lam_re, lam_im, log_step, b_re, b_im, c_re, c_im):
    dt = jnp.exp(log_step)[:, None]
    mag = jnp.exp(lam_re * dt)
    ab_re = mag * jnp.cos(lam_im * dt)
    ab_im = mag * jnp.sin(lam_im * dt)
    den = lam_re * lam_re + lam_im * lam_im
    nr = ab_re - 1.0
    f_re = (nr * lam_re + ab_im * lam_im) / den
    f_im = (ab_im * lam_re - nr * lam_im) / den
    bb_re = f_re[..., None] * b_re - f_im[..., None] * b_im
    bb_im = f_re[..., None] * b_im + f_im[..., None] * b_re

    def blk(x):
        return x.reshape((SSM_BLOCKS, GROUPS_PER_BLOCK) + x.shape[1:])

    def in_map(m_re, m_im):
        return jnp.concatenate([_block_diag(jnp.swapaxes(blk(m_re), 2, 3)),
                                _block_diag(jnp.swapaxes(blk(m_im), 2, 3))], axis=-1)

    def out_map(m_re, m_im):
        return jnp.concatenate([_block_diag(jnp.swapaxes(blk(m_re), 2, 3)),
                                _block_diag(jnp.swapaxes(blk(-m_im), 2, 3))], axis=-2)

    def direct_map(k):
        return _block_diag(jnp.swapaxes(blk(k), 2, 3))

    def cmul(x_re, x_im, y_re, y_im):
        return x_re * y_re - x_im * y_im, x_re * y_im + x_im * y_re

    pw = [(jnp.ones_like(ab_re), jnp.zeros_like(ab_im))]
    for _ in range(S5_UNROLL):
        pw.append(cmul(pw[-1][0], pw[-1][1], ab_re, ab_im))

    def re_prod(m_re, m_im, n_re, n_im):
        hi = lax.Precision.HIGHEST
        return (jnp.einsum("gcp,gpd->gcd", m_re, n_re, precision=hi)
                - jnp.einsum("gcp,gpd->gcd", m_im, n_im, precision=hi))

    a_b = [cmul(p_re[..., None], p_im[..., None], bb_re, bb_im) for p_re, p_im in pw[:S5_UNROLL]]
    c_a = [cmul(c_re, c_im, p_re[:, None, :], p_im[:, None, :]) for p_re, p_im in pw]
    k = [direct_map(re_prod(c_a[d][0], c_a[d][1], bb_re, bb_im)) for d in range(S5_UNROLL)]
    bmap = jnp.concatenate([in_map(*a_b[S5_UNROLL - 1 - j]) for j in range(S5_UNROLL)], axis=-2).astype(BF16)
    cmap = jnp.concatenate([out_map(*c_a[j + 1]) for j in range(S5_UNROLL)], axis=-1).astype(BF16)
    tmap = jnp.concatenate([
        jnp.concatenate([k[j - i] if j >= i else jnp.zeros_like(k[0]) for j in range(S5_UNROLL)], axis=-1)
        for i in range(S5_UNROLL)], axis=-2).astype(BF16)
    apow = jnp.concatenate([blk(pw[S5_UNROLL][0]).reshape(SSM_BLOCKS, BLOCK_STATE),
                            blk(pw[S5_UNROLL][1]).reshape(SSM_BLOCKS, BLOCK_STATE)], axis=-1).reshape(1, STATE_COLS)
    return apow, bmap, cmap, tmap


def _state_to_cols(s_re, s_im):
    nb = s_re.shape[0]
    re = s_re.reshape(nb, SSM_BLOCKS, BLOCK_STATE)
    im = s_im.reshape(nb, SSM_BLOCKS, BLOCK_STATE)
    return jnp.stack([re, im], axis=2).reshape(nb, STATE_COLS)


def _cols_to_state(s):
    nb = s.shape[0]
    s = s.reshape(nb, SSM_BLOCKS, 2, GROUPS_PER_BLOCK, SSM_STATE)
    re = s[:, :, 0].reshape(nb, SSM_GROUPS, SSM_STATE)
    im = s[:, :, 1].reshape(nb, SSM_GROUPS, SSM_STATE)
    return re, im


def _layer(x, s0, hist0, p, *, pos0, tt, tc):
    nb, t_len, _ = x.shape
    h = _ffn(x, p["ffn1_pre_g"], p["ffn1_w_in"], p["ffn1_w_out"], p["ffn1_half_post_g"],
             nb=nb, t_len=t_len, tt=tt, rows_in=True, name="ffn1")
    h, s_fin, hist = _mix(h, s0, hist0, p, pos0=pos0, nb=nb, tc=tc, name="mix")
    y = _ffn(h, p["ffn2_pre_g"], p["ffn2_w_in"], p["ffn2_w_out"], p["ffn2_half_post_g"],
             nb=nb, t_len=t_len, tt=tt, rows_in=False, name="ffn2")
    s_re, s_im = _cols_to_state(s_fin)
    pool = jnp.swapaxes(hist.reshape(POOL_MAXW, nb, D_POOL)[1:], 0, 1)
    return y, s_re, s_im, pool


def kernel(x_prompt, x_sample, state_ssm_re, state_ssm_im, cache_pool, ffn1_pre_g, ffn1_w_in, ffn1_w_out, ffn1_post_g, mix_pre_g, w_in, b_in, ssm_lambda_re, ssm_lambda_im, ssm_log_step, ssm_b_re, ssm_b_im, ssm_c_re, ssm_c_im, ssm_d, ssm_w_glu, ssm_b_glu, pool_w, pool_scale, w_out, mix_post_g, ffn2_pre_g, ffn2_w_in, ffn2_w_out, ffn2_post_g):
    depth = ffn1_w_in.shape[0]
    nb_p = x_prompt.shape[0]
    nb_s, t_s, _ = x_sample.shape

    hp, hs = x_prompt, x_sample
    outs = [[] for _ in range(6)]
    for l in range(depth):
        apow, bmap, cmap, tmap = _ssm_params(ssm_lambda_re[l], ssm_lambda_im[l], ssm_log_step[l],
                                ssm_b_re[l], ssm_b_im[l], ssm_c_re[l], ssm_c_im[l])
        p = {
            "ffn1_pre_g": ffn1_pre_g[l][None], "ffn1_w_in": ffn1_w_in[l].astype(BF16),
            "ffn1_w_out": ffn1_w_out[l].astype(BF16), "ffn1_half_post_g": 0.5 * ffn1_post_g[l][None],
            "mix_pre_g": mix_pre_g[l][None], "w_in": w_in[l].astype(BF16), "b_in": b_in[l][None],
            "apow": apow, "bmap": bmap, "cmap": cmap, "tmap": tmap, "ssm_d": ssm_d[l][None],
            "ssm_w_glu": ssm_w_glu[l].astype(BF16), "ssm_b_glu": ssm_b_glu[l][None],
            "pool_w": pool_w[l].astype(BF16), "pool_scale": pool_scale[l][None],
            "w_out": w_out[l].astype(BF16), "mix_post_g": mix_post_g[l][None],
            "ffn2_pre_g": ffn2_pre_g[l][None], "ffn2_w_in": ffn2_w_in[l].astype(BF16),
            "ffn2_w_out": ffn2_w_out[l].astype(BF16), "ffn2_half_post_g": 0.5 * ffn2_post_g[l][None],
        }
        s0 = jnp.zeros((nb_p, STATE_COLS), F32)
        hist0 = jnp.zeros((POOL_MAXW * nb_p, D_POOL), F32)
        hp, s_re, s_im, pool = _layer(hp, s0, hist0, p, pos0=0, tt=128, tc=128)
        for lst, val in zip(outs[:3], (s_re, s_im, pool)):
            lst.append(val)
        s0 = _state_to_cols(state_ssm_re[l], state_ssm_im[l])
        hist0 = jnp.pad(jnp.swapaxes(cache_pool[l], 0, 1), ((1, 0), (0, 0), (0, 0))).reshape(
            POOL_MAXW * nb_s, D_POOL)
        hs, s_re, s_im, pool = _layer(hs, s0, hist0, p, pos0=PAST_LEN, tt=t_s, tc=t_s)
        for lst, val in zip(outs[3:], (s_re, s_im, pool)):
            lst.append(val)
    return (hp, hs) + tuple(jnp.stack(o) for o in outs)
```
